```python
import jax
import jax.numpy as jnp
from jax import lax
import numpy as np

D_MODEL = 2048
BATCH = 16
SEQ = 2048
DEPTH = 4

N_EVEN = (DEPTH + 1) // 2
N_ODD = DEPTH // 2
N_VRES = N_EVEN - 1

D_A = D_MODEL // 2
N_A = 64
H_A = D_A // N_A
R_W = 64
R_A = 64
R_V = 32
R_G = 160
RWKV_GN_EPS = 64e-5
C_A = 3 * D_A + R_W + R_A + R_G
RWKV_SPLITS = (D_A, 2 * D_A, 3 * D_A, 3 * D_A + R_W, 3 * D_A + R_W + R_A)

D_B = D_MODEL
P_B = 64
H_B = D_B // P_B
N_B = 128
G_B = 4
J_B = H_B // G_B
CONV_K = 4
C_CONV = D_B + 2 * G_B * N_B
C_B = D_B + C_CONV + H_B
CHUNK = 128
MAMBA_NORM_EPS = 1e-5

C_HY = C_A + C_B
D_CAT = D_A + D_B

BS_C = 256
D_RNN = ((4 * D_MODEL // 3 + BS_C // 2) // BS_C) * BS_C
H_C = D_RNN // BS_C
C_RG = 8.0

N_GROUPS_E = 8
E_PER_GROUP = 8
N_EXPERTS = N_GROUPS_E * E_PER_GROUP
TOP_K_E = 2
D_EXPERT = D_MODEL // 8
MOE_BLOCK = 128

NORM_EPS = 1e-6
F32 = jnp.float32

kernel_name = 'hybrid_rwkv7_ssd_rglru_hmoe'


def rms_norm(x, g, eps=NORM_EPS):
    xf = x.astype(F32)
    y = xf * lax.rsqrt(jnp.mean(xf * xf, axis=-1, keepdims=True) + eps)
    return (y * g.astype(F32)).astype(x.dtype)


def token_shift(p):
    return jnp.pad(p, ((0, 0), (1, 0), (0, 0)))[:, :-1]


def causal_conv(x, w, b):
    k = w.shape[0]
    y = lax.conv_general_dilated(
        x, w[:, None, :].astype(x.dtype), window_strides=(1,), padding=[(k - 1, 0)],
        dimension_numbers=('NWC', 'WIO', 'NWC'), feature_group_count=x.shape[-1])
    return y + b.astype(x.dtype)


def rwkv7_scan(r, decay, k, v, kk, a):
    bsz, _, h, n = r.shape

    def step(state, inp):
        r_t, w_t, k_t, v_t, kk_t, a_t = inp
        sa = jnp.einsum('bhvk,bhk->bhv', state, -kk_t)
        state = (state * w_t[:, :, None, :]
                 + sa[..., None] * (kk_t * a_t)[:, :, None, :]
                 + v_t[..., None] * k_t[:, :, None, :])
        return state, jnp.einsum('bhvk,bhk->bhv', state, r_t)

    xs = tuple(jnp.moveaxis(t, 1, 0) for t in (r, decay, k, v, kk, a))
    _, ys = lax.scan(step, jnp.zeros((bsz, h, n, n), F32), xs)
    return jnp.moveaxis(ys, 0, 1)


def rwkv7_heads(pa, mu, w_up, a_up, g_up, vecs, r_k, v_first, vres):
    bsz, seq, _ = pa.shape
    pa = pa.astype(F32)
    pa = pa + mu * (token_shift(pa) - pa)
    r, k, v, dw, da, dg = jnp.split(pa, RWKV_SPLITS, axis=-1)
    w0, a0, k_k, k_a, lnx_w, lnx_b = (vecs[i] for i in range(6))
    w_log = -jax.nn.softplus(-(w0 + jnp.tanh(dw) @ w_up)) - 0.5
    decay = jnp.exp(-jnp.exp(w_log))
    a = jax.nn.sigmoid(a0 + da @ a_up)
    g = jax.nn.sigmoid(dg) @ g_up
    if vres is None:
        v_first = v
    else:
        dv, mu_v, v_up, v0 = vres
        dv = dv.astype(F32)
        dv = dv + mu_v * (token_shift(dv) - dv)
        v = v + (v_first - v) * jax.nn.sigmoid(v0 + dv @ v_up)
    heads = lambda t: t.reshape(bsz, seq, H_A, N_A)
    kk = heads(k * k_k)
    kk = kk / jnp.maximum(jnp.linalg.norm(kk, axis=-1, keepdims=True), 1e-12)
    k = k * (1.0 + (a - 1.0) * k_a)
    rh, kh, vh = heads(r), heads(k), heads(v)
    y = rwkv7_scan(rh, heads(decay), kh, vh, kk, heads(a))
    mean = jnp.mean(y, axis=-1, keepdims=True)
    var = jnp.mean(jnp.square(y - mean), axis=-1, keepdims=True)
    y = ((y - mean) * lax.rsqrt(var + RWKV_GN_EPS)).reshape(bsz, seq, D_A) * lnx_w + lnx_b
    bonus = jnp.sum(rh * kh * r_k, axis=-1, keepdims=True) * vh
    y = (y + bonus.reshape(bsz, seq, D_A)) * g
    return y, v_first


def ssd_chunked(x, dt, a_neg, bm, cm):
    bsz, seq = x.shape[:2]
    nc = seq // CHUNK

    def chunks(t):
        return jnp.moveaxis(t.reshape((bsz, nc, CHUNK) + t.shape[2:]), 1, 0)

    causal = jnp.tril(jnp.ones((CHUNK, CHUNK), bool))[None, :, :, None, None]

    def step(state, inp):
        x_c, la_c, b_c, c_c = inp
        cum = jnp.cumsum(la_c, axis=1)
        seg = cum[:, :, None] - cum[:, None, :]
        decay = jnp.exp(jnp.where(causal, seg, -jnp.inf))
        scores = jnp.einsum('btgn,bsgn->btsg', c_c, b_c)[..., None] * decay
        y = jnp.einsum('btsgj,bsgjp->btgjp', scores, x_c)
        y = y + jnp.einsum('btgn,bgjpn->btgjp', c_c, state) * jnp.exp(cum)[..., None]
        decay_end = jnp.exp(cum[:, -1:] - cum)
        state = (state * jnp.exp(cum[:, -1])[..., None, None]
                 + jnp.einsum('bsgn,bsgj,bsgjp->bgjpn', b_c, decay_end, x_c))
        return state, y

    xs = (chunks(x.astype(F32) * dt[..., None]), chunks(dt * a_neg),
          chunks(bm.astype(F32)), chunks(cm.astype(F32)))
    s0 = jnp.zeros((bsz, G_B, J_B, P_B, N_B), F32)
    _, ys = lax.scan(step, s0, xs)
    return jnp.moveaxis(ys, 0, 1).reshape(x.shape)


def mamba2_heads(pb, conv_w, conv_b, head, norm_g):
    bsz, seq, _ = pb.shape
    z, xbc, dt = jnp.split(pb, [D_B, D_B + C_CONV], axis=-1)
    xbc = jax.nn.silu(causal_conv(xbc, conv_w, conv_b))
    xs, bm, cm = jnp.split(xbc, [D_B, D_B + G_B * N_B], axis=-1)
    dt_bias, a_log, d_skip = head[0], head[1], head[2]
    dt = jax.nn.softplus(dt.astype(F32) + dt_bias)
    a_neg = -jnp.exp(a_log.astype(F32)).reshape(G_B, J_B)
    xh = xs.reshape(bsz, seq, G_B, J_B, P_B).astype(F32)
    y = ssd_chunked(xh, dt.reshape(bsz, seq, G_B, J_B), a_neg,
                    bm.reshape(bsz, seq, G_B, N_B), cm.reshape(bsz, seq, G_B, N_B))
    y = y + d_skip.reshape(G_B, J_B)[..., None] * xh
    y = y.reshape(bsz, seq, D_B) * jax.nn.silu(z.astype(F32))
    yg = y.reshape(bsz, seq, G_B, D_B // G_B)
    yg = yg * lax.rsqrt(jnp.mean(yg * yg, axis=-1, keepdims=True) + MAMBA_NORM_EPS)
    return yg.reshape(bsz, seq, D_B) * norm_g


def hybrid_mixer(h, w_in, mu, w_up, a_up, g_up, vecs, r_k, conv_w, conv_b, head, mb_norm,
                 w_out, v_first, vres_params):
    p = h @ w_in
    pa, pb, dv = p[..., :C_A], p[..., C_A:C_HY], p[..., C_HY:]
    vres = None if vres_params is None else (dv,) + vres_params
    ya, v_first = rwkv7_heads(pa, mu, w_up, a_up, g_up, vecs, r_k, v_first, vres)
    yb = mamba2_heads(pb, conv_w, conv_b, head, mb_norm)
    y = jnp.concatenate([ya, yb], axis=-1).astype(h.dtype) @ w_out
    return y.astype(h.dtype), v_first


def linear_scan(a, b):
    def combine(left, right):
        a_l, b_l = left
        a_r, b_r = right
        return a_l * a_r, a_r * b_l + b_r
    _, hs = lax.associative_scan(combine, (a, b), axis=1)
    return hs


def rglru_mixer(h, w_in, conv_w, conv_b, gate_w, gate_b, lam, w_out):
    bsz, seq, _ = h.shape
    p = h @ w_in
    gate, xr = jnp.split(p, 2, axis=-1)
    xr = causal_conv(xr, conv_w, conv_b)
    xb = xr.reshape(bsz, seq, H_C, BS_C).astype(F32)
    gates = jax.nn.sigmoid(jnp.einsum('bshi,ghij->gbshj', xb, gate_w)
                           + gate_b.reshape(2, 1, 1, H_C, BS_C))
    r_gate, i_gate = gates[0], gates[1]
    log_a = -C_RG * r_gate * jax.nn.softplus(-lam.reshape(H_C, BS_C))
    a = jnp.exp(log_a)
    mult = jnp.sqrt(-jnp.expm1(2.0 * log_a))
    mult = jnp.where((jnp.arange(seq) == 0)[None, :, None, None], 1.0, mult)
    hs = linear_scan(a, mult * (i_gate * xb))
    y = hs.reshape(bsz, seq, D_RNN).astype(h.dtype) * jax.nn.gelu(gate)
    return (y @ w_out).astype(h.dtype)


def hier_moe(h, w_grp, b_grp, w_exp, b_exp, w_gate, w_up, w_down):
    bsz, seq, d = h.shape
    t = bsz * seq
    xt = h.reshape(t, d)
    g_logits = (xt @ w_grp).astype(F32) + b_grp.astype(F32)
    g_idx = jnp.argmax(g_logits, axis=-1)
    g_w = jnp.take_along_axis(jax.nn.softmax(g_logits, axis=-1), g_idx[:, None], axis=-1)
    e_logits = ((xt @ w_exp).astype(F32) + b_exp.astype(F32)).reshape(t, N_GROUPS_E, E_PER_GROUP)
    e_logits = jnp.take_along_axis(e_logits, g_idx[:, None, None], axis=1)[:, 0]
    e_top, e_loc = lax.top_k(e_logits, TOP_K_E)
    e_w = jax.nn.softmax(e_top, axis=-1) * g_w
    e_flat = (g_idx[:, None] * E_PER_GROUP + e_loc).reshape(-1).astype(jnp.int32)
    w_flat = e_w.reshape(-1)
    tok_flat = jnp.repeat(jnp.arange(t, dtype=jnp.int32), TOP_K_E)
    n_assign = t * TOP_K_E
    order = jnp.argsort(e_flat)
    e_sorted = e_flat[order]
    counts = jnp.bincount(e_flat, length=N_EXPERTS)
    padded = (counts + MOE_BLOCK - 1) // MOE_BLOCK * MOE_BLOCK
    raw_start = jnp.cumsum(counts) - counts
    pad_start = jnp.cumsum(padded) - padded
    dest = pad_start[e_sorted] + (jnp.arange(n_assign, dtype=jnp.int32) - raw_start[e_sorted])
    n_slots = n_assign + N_EXPERTS * MOE_BLOCK
    n_blocks = n_slots // MOE_BLOCK
    slot_tok = jnp.full((n_slots,), t, jnp.int32).at[dest].set(tok_flat[order])
    slot_w = jnp.zeros((n_slots,), F32).at[dest].set(w_flat[order])
    block_start = jnp.arange(n_blocks, dtype=jnp.int32) * MOE_BLOCK
    block_exp = jnp.minimum(
        jnp.sum(block_start[:, None] >= (pad_start + padded)[None, :], axis=-1), N_EXPERTS - 1)
    x_pad = jnp.concatenate([xt, jnp.zeros((1, d), xt.dtype)], axis=0)

    def run_block(args):
        e, toks = args
        xb = x_pad[toks]
        hb = jax.nn.silu(xb @ w_gate[e]) * (xb @ w_up[e])
        return hb @ w_down[e]

    y = lax.map(run_block, (block_exp, slot_tok.reshape(n_blocks, MOE_BLOCK)))
    y = y.reshape(n_slots, d).astype(F32) * slot_w[:, None]
    out = jnp.zeros((t + 1, d), F32).at[slot_tok].add(y)[:t]
    return out.reshape(bsz, seq, d).astype(h.dtype)


def setup_inputs(seed: int = 0) -> dict:
    key = jax.random.key(seed)
    ks = iter(jax.random.split(key, 64))

    def nrm(shape, scale):
        return jax.random.normal(next(ks), shape, F32) * scale

    def unif(shape, lo, hi):
        return jax.random.uniform(next(ks), shape, F32, lo, hi)

    NE, NO, NV = N_EVEN, N_ODD, N_VRES
    x = nrm((BATCH, SEQ, D_MODEL), 1.0)
    norm_mix = 1.0 + nrm((DEPTH, D_MODEL), 0.02)
    norm_ffn = 1.0 + nrm((DEPTH, D_MODEL), 0.02)
    norm_final = 1.0 + nrm((D_MODEL,), 0.02)
    hy_w_in_first = nrm((D_MODEL, C_HY), D_MODEL ** -0.5)
    hy_w_in_vres = nrm((NV, D_MODEL, C_HY + R_V), D_MODEL ** -0.5)
    rk_mu = unif((NE, C_A), 0.0, 1.0)
    rk_mu_v = unif((NV, R_V), 0.0, 1.0)
    rk_w_up = nrm((NE, R_W, D_A), 0.1 * R_W ** -0.5)
    rk_a_up = nrm((NE, R_A, D_A), R_A ** -0.5)
    rk_g_up = nrm((NE, R_G, D_A), R_G ** -0.5)
    w0 = unif((NE, D_A), -6.0, -1.0)
    a0 = nrm((NE, D_A), 0.1)
    k_k = 0.85 + nrm((NE, D_A), 0.02)
    k_a = 1.0 + nrm((NE, D_A), 0.02)
    lnx_w = 1.0 + nrm((NE, D_A), 0.02)
    lnx_b = nrm((NE, D_A), 0.02)
    rk_vecs = jnp.stack([w0, a0, k_k, k_a, lnx_w, lnx_b], axis=1)
    rk_r_k = nrm((NE, H_A, N_A), 0.1)
    rk_v_up = nrm((NV, R_V, D_A), R_V ** -0.5)
    rk_v0 = nrm((NV, D_A), 0.1)
    mb_conv_w = nrm((NE, CONV_K, C_CONV), CONV_K ** -0.5)
    mb_conv_b = nrm((NE, C_CONV), 0.02)
    dt0 = jnp.exp(unif((NE, H_B), float(np.log(1e-3)), float(np.log(1e-1))))
    dt_bias = dt0 + jnp.log(-jnp.expm1(-dt0))
    a_log = jnp.log(unif((NE, H_B), 1.0, 16.0))
    d_skip = 1.0 + nrm((NE, H_B), 0.02)
    mb_head = jnp.stack([dt_bias, a_log, d_skip], axis=1)
    mb_norm = 1.0 + nrm((NE, D_B), 0.02)
    hy_w_out = nrm((NE, D_CAT, D_MODEL), D_CAT ** -0.5)
    lr_w_in = nrm((NO, D_MODEL, 2 * D_RNN), D_MODEL ** -0.5)
    lr_conv_w = nrm((NO, CONV_K, D_RNN), CONV_K ** -0.5)
    lr_conv_b = nrm((NO, D_RNN), 0.02)
    lr_gate_w = nrm((NO, 2, H_C, BS_C, BS_C), BS_C ** -0.5)
    lr_gate_b = nrm((NO, 2, D_RNN), 0.02)
    a8 = unif((NO, D_RNN), 0.9, 0.999)
    s = a8 ** (1.0 / C_RG)
    lr_lambda = jnp.log(s) - jnp.log1p(-s)
    lr_w_out = nrm((NO, D_RNN, D_MODEL), D_RNN ** -0.5)
    moe_w_grp = nrm((DEPTH, D_MODEL, N_GROUPS_E), D_MODEL ** -0.5)
    moe_b_grp = nrm((DEPTH, N_GROUPS_E), 0.01)
    moe_w_exp = nrm((DEPTH, D_MODEL, N_EXPERTS), D_MODEL ** -0.5)
    moe_b_exp = nrm((DEPTH, N_EXPERTS), 0.01)
    moe_w_gate = nrm((DEPTH, N_EXPERTS, D_MODEL, D_EXPERT), D_MODEL ** -0.5)
    moe_w_up = nrm((DEPTH, N_EXPERTS, D_MODEL, D_EXPERT), D_MODEL ** -0.5)
    moe_w_down = nrm((DEPTH, N_EXPERTS, D_EXPERT, D_MODEL), D_EXPERT ** -0.5)
    return {
        'x': x, 'norm_mix': norm_mix, 'norm_ffn': norm_ffn, 'norm_final': norm_final,
        'hy_w_in_first': hy_w_in_first, 'hy_w_in_vres': hy_w_in_vres,
        'rk_mu': rk_mu, 'rk_mu_v': rk_mu_v, 'rk_w_up': rk_w_up, 'rk_a_up': rk_a_up,
        'rk_g_up': rk_g_up, 'rk_vecs': rk_vecs, 'rk_r_k': rk_r_k, 'rk_v_up': rk_v_up, 'rk_v0': rk_v0,
        'mb_conv_w': mb_conv_w, 'mb_conv_b': mb_conv_b, 'mb_head': mb_head, 'mb_norm': mb_norm,
        'hy_w_out': hy_w_out,
        'lr_w_in': lr_w_in, 'lr_conv_w': lr_conv_w, 'lr_conv_b': lr_conv_b, 'lr_gate_w': lr_gate_w,
        'lr_gate_b': lr_gate_b, 'lr_lambda': lr_lambda, 'lr_w_out': lr_w_out,
        'moe_w_grp': moe_w_grp, 'moe_b_grp': moe_b_grp, 'moe_w_exp': moe_w_exp, 'moe_b_exp': moe_b_exp,
        'moe_w_gate': moe_w_gate, 'moe_w_up': moe_w_up, 'moe_w_down': moe_w_down,
    }


def reference(x, norm_mix, norm_ffn, norm_final,
              hy_w_in_first, hy_w_in_vres,
              rk_mu, rk_mu_v, rk_w_up, rk_a_up, rk_g_up, rk_vecs, rk_r_k, rk_v_up, rk_v0,
              mb_conv_w, mb_conv_b, mb_head, mb_norm, hy_w_out,
              lr_w_in, lr_conv_w, lr_conv_b, lr_gate_w, lr_gate_b, lr_lambda, lr_w_out,
              moe_w_grp, moe_b_grp, moe_w_exp, moe_b_exp, moe_w_gate, moe_w_up, moe_w_down):
    h = x
    v_first = None
    for layer in range(DEPTH):
        hn = rms_norm(h, norm_mix[layer])
        j = layer // 2
        if layer % 2 == 0:
            if j == 0:
                w_in, vres_params = hy_w_in_first, None
            else:
                w_in = hy_w_in_vres[j - 1]
                vres_params = (rk_mu_v[j - 1], rk_v_up[j - 1], rk_v0[j - 1])
            mix, v_first = hybrid_mixer(
                hn, w_in, rk_mu[j], rk_w_up[j], rk_a_up[j], rk_g_up[j], rk_vecs[j], rk_r_k[j],
                mb_conv_w[j], mb_conv_b[j], mb_head[j], mb_norm[j], hy_w_out[j], v_first, vres_params)
        else:
            mix = rglru_mixer(hn, lr_w_in[j], lr_conv_w[j], lr_conv_b[j], lr_gate_w[j],
                              lr_gate_b[j], lr_lambda[j], lr_w_out[j])
        h = h + mix
        h = h + hier_moe(rms_norm(h, norm_ffn[layer]), moe_w_grp[layer], moe_b_grp[layer],
                         moe_w_exp[layer], moe_b_exp[layer], moe_w_gate[layer],
                         moe_w_up[layer], moe_w_down[layer])
    return rms_norm(h, norm_final)
```

```python
import functools

import jax
import jax.numpy as jnp
from jax import lax
from jax.experimental import pallas as pl
from jax.experimental.pallas import tpu as pltpu

D_MODEL = 2048
DEPTH = 4
D_A = D_MODEL // 2
N_A = 64
H_A = D_A // N_A
R_W = 64
R_A = 64
R_V = 32
R_G = 160
RWKV_GN_EPS = 64e-5
C_A = 3 * D_A + R_W + R_A + R_G
RWKV_SPLITS = (D_A, 2 * D_A, 3 * D_A, 3 * D_A + R_W, 3 * D_A + R_W + R_A)
D_B = D_MODEL
P_B = 64
H_B = D_B // P_B
N_B = 128
G_B = 4
J_B = H_B // G_B
CONV_K = 4
C_CONV = D_B + 2 * G_B * N_B
C_B = D_B + C_CONV + H_B
CHUNK = 128
MAMBA_NORM_EPS = 1e-5
C_HY = C_A + C_B
D_CAT = D_A + D_B
BS_C = 256
D_RNN = ((4 * D_MODEL // 3 + BS_C // 2) // BS_C) * BS_C
H_C = D_RNN // BS_C
C_RG = 8.0
N_GROUPS_E = 8
E_PER_GROUP = 8
N_EXPERTS = N_GROUPS_E * E_PER_GROUP
TOP_K_E = 2
D_EXPERT = D_MODEL // 8
MOE_BLOCK = 128
NORM_EPS = 1e-6
F32 = jnp.float32
BF16 = jnp.bfloat16

VMEM_LIMIT_BYTES = 56 * 1024 * 1024


def _round_up(n, m):
    return (n + m - 1) // m * m


def _mm_kernel(x_ref, w_ref, o_ref):
    o_ref[...] = jnp.dot(x_ref[...].astype(BF16), w_ref[...], preferred_element_type=F32)


def matmul(x, w, tm=512, tn=512):
    m, k = x.shape
    n = w.shape[1]
    n_pad = _round_up(n, tn)
    wb = w.astype(BF16)
    if n_pad != n:
        wb = jnp.pad(wb, ((0, 0), (0, n_pad - n)))
    out = pl.pallas_call(
        _mm_kernel,
        out_shape=jax.ShapeDtypeStruct((m, n_pad), F32),
        grid=(m // tm, n_pad // tn),
        in_specs=[pl.BlockSpec((tm, k), lambda i, j: (i, 0)),
                  pl.BlockSpec((k, tn), lambda i, j: (0, j))],
        out_specs=pl.BlockSpec((tm, tn), lambda i, j: (i, j)),
        compiler_params=pltpu.CompilerParams(
            dimension_semantics=("parallel", "arbitrary"), vmem_limit_bytes=VMEM_LIMIT_BYTES),
        name="matmul",
    )(x, wb)
    return out[:, :n] if n_pad != n else out


def rms_norm(x, g, eps=NORM_EPS):
    xf = x.astype(F32)
    y = xf * lax.rsqrt(jnp.mean(xf * xf, axis=-1, keepdims=True) + eps)
    return (y * g.astype(F32)).astype(x.dtype)


def token_shift(p):
    return jnp.pad(p, ((0, 0), (1, 0), (0, 0)))[:, :-1]


def causal_conv(x, w, b):
    k = w.shape[0]
    y = lax.conv_general_dilated(
        x, w[:, None, :].astype(x.dtype), window_strides=(1,), padding=[(k - 1, 0)],
        dimension_numbers=('NWC', 'WIO', 'NWC'), feature_group_count=x.shape[-1])
    return y + b.astype(x.dtype)


def rwkv7_scan(r, decay, k, v, kk, a):
    bsz, _, h, n = r.shape

    def step(state, inp):
        r_t, w_t, k_t, v_t, kk_t, a_t = inp
        sa = jnp.einsum('bhvk,bhk->bhv', state, -kk_t)
        state = (state * w_t[:, :, None, :]
                 + sa[..., None] * (kk_t * a_t)[:, :, None, :]
                 + v_t[..., None] * k_t[:, :, None, :])
        return state, jnp.einsum('bhvk,bhk->bhv', state, r_t)

    xs = tuple(jnp.moveaxis(t, 1, 0) for t in (r, decay, k, v, kk, a))
    _, ys = lax.scan(step, jnp.zeros((bsz, h, n, n), F32), xs)
    return jnp.moveaxis(ys, 0, 1)


def rwkv7_heads(pa, mu, w_up, a_up, g_up, vecs, r_k, v_first, vres):
    bsz, seq, _ = pa.shape
    pa = pa.astype(F32)
    pa = pa + mu * (token_shift(pa) - pa)
    r, k, v, dw, da, dg = jnp.split(pa, RWKV_SPLITS, axis=-1)
    w0, a0, k_k, k_a, lnx_w, lnx_b = (vecs[i] for i in range(6))
    w_log = -jax.nn.softplus(-(w0 + jnp.tanh(dw) @ w_up)) - 0.5
    decay = jnp.exp(-jnp.exp(w_log))
    a = jax.nn.sigmoid(a0 + da @ a_up)
    g = jax.nn.sigmoid(dg) @ g_up
    if vres is None:
        v_first = v
    else:
        dv, mu_v, v_up, v0 = vres
        dv = dv.astype(F32)
        dv = dv + mu_v * (token_shift(dv) - dv)
        v = v + (v_first - v) * jax.nn.sigmoid(v0 + dv @ v_up)
    heads = lambda t: t.reshape(bsz, seq, H_A, N_A)
    kk = heads(k * k_k)
    kk = kk / jnp.maximum(jnp.linalg.norm(kk, axis=-1, keepdims=True), 1e-12)
    k = k * (1.0 + (a - 1.0) * k_a)
    rh, kh, vh = heads(r), heads(k), heads(v)
    y = rwkv7_scan(rh, heads(decay), kh, vh, kk, heads(a))
    mean = jnp.mean(y, axis=-1, keepdims=True)
    var = jnp.mean(jnp.square(y - mean), axis=-1, keepdims=True)
    y = ((y - mean) * lax.rsqrt(var + RWKV_GN_EPS)).reshape(bsz, seq, D_A) * lnx_w + lnx_b
    bonus = jnp.sum(rh * kh * r_k, axis=-1, keepdims=True) * vh
    y = (y + bonus.reshape(bsz, seq, D_A)) * g
    return y, v_first


def ssd_chunked(x, dt, a_neg, bm, cm):
    bsz, seq = x.shape[:2]
    nc = seq // CHUNK

    def chunks(t):
        return jnp.moveaxis(t.reshape((bsz, nc, CHUNK) + t.shape[2:]), 1, 0)

    causal = jnp.tril(jnp.ones((CHUNK, CHUNK), bool))[None, :, :, None, None]

    def step(state, inp):
        x_c, la_c, b_c, c_c = inp
        cum = jnp.cumsum(la_c, axis=1)
        seg = cum[:, :, None] - cum[:, None, :]
        decay = jnp.exp(jnp.where(causal, seg, -jnp.inf))
        scores = jnp.einsum('btgn,bsgn->btsg', c_c, b_c)[..., None] * decay
        y = jnp.einsum('btsgj,bsgjp->btgjp', scores, x_c)
        y = y + jnp.einsum('btgn,bgjpn->btgjp', c_c, state) * jnp.exp(cum)[..., None]
        decay_end = jnp.exp(cum[:, -1:] - cum)
        state = (state * jnp.exp(cum[:, -1])[..., None, None]
                 + jnp.einsum('bsgn,bsgj,bsgjp->bgjpn', b_c, decay_end, x_c))
        return state, y

    xs = (chunks(x.astype(F32) * dt[..., None]), chunks(dt * a_neg),
          chunks(bm.astype(F32)), chunks(cm.astype(F32)))
    s0 = jnp.zeros((bsz, G_B, J_B, P_B, N_B), F32)
    _, ys = lax.scan(step, s0, xs)
    return jnp.moveaxis(ys, 0, 1).reshape(x.shape)


def mamba2_heads(pb, conv_w, conv_b, head, norm_g):
    bsz, seq, _ = pb.shape
    z, xbc, dt = jnp.split(pb, [D_B, D_B + C_CONV], axis=-1)
    xbc = jax.nn.silu(causal_conv(xbc, conv_w, conv_b))
    xs, bm, cm = jnp.split(xbc, [D_B, D_B + G_B * N_B], axis=-1)
    dt_bias, a_log, d_skip = head[0], head[1], head[2]
    dt = jax.nn.softplus(dt.astype(F32) + dt_bias)
    a_neg = -jnp.exp(a_log.astype(F32)).reshape(G_B, J_B)
    xh = xs.reshape(bsz, seq, G_B, J_B, P_B).astype(F32)
    y = ssd_chunked(xh, dt.reshape(bsz, seq, G_B, J_B), a_neg,
                    bm.reshape(bsz, seq, G_B, N_B), cm.reshape(bsz, seq, G_B, N_B))
    y = y + d_skip.reshape(G_B, J_B)[..., None] * xh
    y = y.reshape(bsz, seq, D_B) * jax.nn.silu(z.astype(F32))
    yg = y.reshape(bsz, seq, G_B, D_B // G_B)
    yg = yg * lax.rsqrt(jnp.mean(yg * yg, axis=-1, keepdims=True) + MAMBA_NORM_EPS)
    return yg.reshape(bsz, seq, D_B) * norm_g


def hybrid_mixer(h, w_in, mu, w_up, a_up, g_up, vecs, r_k, conv_w, conv_b, head, mb_norm,
                 w_out, v_first, vres_params):
    bsz, seq, d = h.shape
    p = matmul(h.reshape(bsz * seq, d), w_in).reshape(bsz, seq, -1)
    pa, pb, dv = p[..., :C_A], p[..., C_A:C_HY], p[..., C_HY:]
    vres = None if vres_params is None else (dv,) + vres_params
    ya, v_first = rwkv7_heads(pa, mu, w_up, a_up, g_up, vecs, r_k, v_first, vres)
    yb = mamba2_heads(pb, conv_w, conv_b, head, mb_norm)
    y = jnp.concatenate([ya, yb], axis=-1).astype(h.dtype)
    y = matmul(y.reshape(bsz * seq, D_CAT), w_out).reshape(bsz, seq, d)
    return y.astype(h.dtype), v_first


def linear_scan(a, b):
    def combine(left, right):
        a_l, b_l = left
        a_r, b_r = right
        return a_l * a_r, a_r * b_l + b_r
    _, hs = lax.associative_scan(combine, (a, b), axis=1)
    return hs


def rglru_mixer(h, w_in, conv_w, conv_b, gate_w, gate_b, lam, w_out):
    bsz, seq, d = h.shape
    p = matmul(h.reshape(bsz * seq, d), w_in).reshape(bsz, seq, -1)
    gate, xr = jnp.split(p, 2, axis=-1)
    xr = causal_conv(xr, conv_w, conv_b)
    xb = xr.reshape(bsz, seq, H_C, BS_C).astype(F32)
    gates = jax.nn.sigmoid(jnp.einsum('bshi,ghij->gbshj', xb, gate_w)
                           + gate_b.reshape(2, 1, 1, H_C, BS_C))
    r_gate, i_gate = gates[0], gates[1]
    log_a = -C_RG * r_gate * jax.nn.softplus(-lam.reshape(H_C, BS_C))
    a = jnp.exp(log_a)
    mult = jnp.sqrt(-jnp.expm1(2.0 * log_a))
    mult = jnp.where((jnp.arange(seq) == 0)[None, :, None, None], 1.0, mult)
    hs = linear_scan(a, mult * (i_gate * xb))
    y = hs.reshape(bsz, seq, D_RNN).astype(h.dtype) * jax.nn.gelu(gate)
    return matmul(y.reshape(bsz * seq, D_RNN), w_out).reshape(bsz, seq, d).astype(h.dtype)


def hier_moe(h, w_grp, b_grp, w_exp, b_exp, w_gate, w_up, w_down):
    bsz, seq, d = h.shape
    t = bsz * seq
    xt = h.reshape(t, d)
    g_logits = (xt @ w_grp).astype(F32) + b_grp.astype(F32)
    g_idx = jnp.argmax(g_logits, axis=-1)
    g_w = jnp.take_along_axis(jax.nn.softmax(g_logits, axis=-1), g_idx[:, None], axis=-1)
    e_logits = ((xt @ w_exp).astype(F32) + b_exp.astype(F32)).reshape(t, N_GROUPS_E, E_PER_GROUP)
    e_logits = jnp.take_along_axis(e_logits, g_idx[:, None, None], axis=1)[:, 0]
    e_top, e_loc = lax.top_k(e_logits, TOP_K_E)
    e_w = jax.nn.softmax(e_top, axis=-1) * g_w
    e_flat = (g_idx[:, None] * E_PER_GROUP + e_loc).reshape(-1).astype(jnp.int32)
    w_flat = e_w.reshape(-1)
    tok_flat = jnp.repeat(jnp.arange(t, dtype=jnp.int32), TOP_K_E)
    n_assign = t * TOP_K_E
    order = jnp.argsort(e_flat)
    e_sorted = e_flat[order]
    counts = jnp.bincount(e_flat, length=N_EXPERTS)
    padded = (counts + MOE_BLOCK - 1) // MOE_BLOCK * MOE_BLOCK
    raw_start = jnp.cumsum(counts) - counts
    pad_start = jnp.cumsum(padded) - padded
    dest = pad_start[e_sorted] + (jnp.arange(n_assign, dtype=jnp.int32) - raw_start[e_sorted])
    n_slots = n_assign + N_EXPERTS * MOE_BLOCK
    n_blocks = n_slots // MOE_BLOCK
    slot_tok = jnp.full((n_slots,), t, jnp.int32).at[dest].set(tok_flat[order])
    slot_w = jnp.zeros((n_slots,), F32).at[dest].set(w_flat[order])
    block_start = jnp.arange(n_blocks, dtype=jnp.int32) * MOE_BLOCK
    block_exp = jnp.minimum(
        jnp.sum(block_start[:, None] >= (pad_start + padded)[None, :], axis=-1), N_EXPERTS - 1)
    x_pad = jnp.concatenate([xt, jnp.zeros((1, d), xt.dtype)], axis=0)

    def run_block(args):
        e, toks = args
        xb = x_pad[toks]
        hb = jax.nn.silu(xb @ w_gate[e]) * (xb @ w_up[e])
        return hb @ w_down[e]

    y = lax.map(run_block, (block_exp, slot_tok.reshape(n_blocks, MOE_BLOCK)))
    y = y.reshape(n_slots, d).astype(F32) * slot_w[:, None]
    out = jnp.zeros((t + 1, d), F32).at[slot_tok].add(y)[:t]
    return out.reshape(bsz, seq, d).astype(h.dtype)


def kernel(x, norm_mix, norm_ffn, norm_final, hy_w_in_first, hy_w_in_vres, rk_mu, rk_mu_v, rk_w_up, rk_a_up, rk_g_up, rk_vecs, rk_r_k, rk_v_up, rk_v0, mb_conv_w, mb_conv_b, mb_head, mb_norm, hy_w_out, lr_w_in, lr_conv_w, lr_conv_b, lr_gate_w, lr_gate_b, lr_lambda, lr_w_out, moe_w_grp, moe_b_grp, moe_w_exp, moe_b_exp, moe_w_gate, moe_w_up, moe_w_down):
    h = x
    v_first = None
    for layer in range(DEPTH):
        hn = rms_norm(h, norm_mix[layer])
        j = layer // 2
        if layer % 2 == 0:
            if j == 0:
                w_in, vres_params = hy_w_in_first, None
            else:
                w_in = hy_w_in_vres[j - 1]
                vres_params = (rk_mu_v[j - 1], rk_v_up[j - 1], rk_v0[j - 1])
            mix, v_first = hybrid_mixer(
                hn, w_in, rk_mu[j], rk_w_up[j], rk_a_up[j], rk_g_up[j], rk_vecs[j], rk_r_k[j],
                mb_conv_w[j], mb_conv_b[j], mb_head[j], mb_norm[j], hy_w_out[j], v_first, vres_params)
        else:
            mix = rglru_mixer(hn, lr_w_in[j], lr_conv_w[j], lr_conv_b[j], lr_gate_w[j],
                              lr_gate_b[j], lr_lambda[j], lr_w_out[j])
        h = h + mix
        h = h + hier_moe(rms_norm(h, norm_ffn[layer]), moe_w_grp[layer], moe_b_grp[layer],
                         moe_w_exp[layer], moe_b_exp[layer], moe_w_gate[layer],
                         moe_w_up[layer], moe_w_down[layer])
    return rms_norm(h, norm_final)
```

```python
import functools

import jax
import jax.numpy as jnp
from jax import lax
from jax.experimental import pallas as pl
from jax.experimental.pallas import tpu as pltpu

D_MODEL = 2048
DEPTH = 4
D_A = D_MODEL // 2
N_A = 64
H_A = D_A // N_A
R_W = 64
R_A = 64
R_V = 32
R_G = 160
RWKV_GN_EPS = 64e-5
C_A = 3 * D_A + R_W + R_A + R_G
RWKV_SPLITS = (D_A, 2 * D_A, 3 * D_A, 3 * D_A + R_W, 3 * D_A + R_W + R_A)
D_B = D_MODEL
P_B = 64
H_B = D_B // P_B
N_B = 128
G_B = 4
J_B = H_B // G_B
CONV_K = 4
C_CONV = D_B + 2 * G_B * N_B
C_B = D_B + C_CONV + H_B
CHUNK = 128
MAMBA_NORM_EPS = 1e-5
C_HY = C_A + C_B
D_CAT = D_A + D_B
BS_C = 256
D_RNN = ((4 * D_MODEL // 3 + BS_C // 2) // BS_C) * BS_C
H_C = D_RNN // BS_C
C_RG = 8.0
N_GROUPS_E = 8
E_PER_GROUP = 8
N_EXPERTS = N_GROUPS_E * E_PER_GROUP
TOP_K_E = 2
D_EXPERT = D_MODEL // 8
MOE_BLOCK = 128
NORM_EPS = 1e-6
F32 = jnp.float32
BF16 = jnp.bfloat16

VMEM_LIMIT_BYTES = 56 * 1024 * 1024


def _round_up(n, m):
    return (n + m - 1) // m * m


def _mm_kernel(x_ref, w_ref, o_ref):
    o_ref[...] = jnp.dot(x_ref[...].astype(BF16), w_ref[...], preferred_element_type=F32)


def matmul(x, w, tm=512, tn=512):
    m, k = x.shape
    n = w.shape[1]
    n_pad = _round_up(n, tn)
    wb = w.astype(BF16)
    if n_pad != n:
        wb = jnp.pad(wb, ((0, 0), (0, n_pad - n)))
    out = pl.pallas_call(
        _mm_kernel,
        out_shape=jax.ShapeDtypeStruct((m, n_pad), F32),
        grid=(m // tm, n_pad // tn),
        in_specs=[pl.BlockSpec((tm, k), lambda i, j: (i, 0)),
                  pl.BlockSpec((k, tn), lambda i, j: (0, j))],
        out_specs=pl.BlockSpec((tm, tn), lambda i, j: (i, j)),
        compiler_params=pltpu.CompilerParams(
            dimension_semantics=("parallel", "arbitrary"), vmem_limit_bytes=VMEM_LIMIT_BYTES),
        name="matmul",
    )(x, wb)
    return out[:, :n] if n_pad != n else out


def rms_norm(x, g, eps=NORM_EPS):
    xf = x.astype(F32)
    y = xf * lax.rsqrt(jnp.mean(xf * xf, axis=-1, keepdims=True) + eps)
    return (y * g.astype(F32)).astype(x.dtype)


def token_shift(p):
    return jnp.pad(p, ((0, 0), (1, 0), (0, 0)))[:, :-1]


def causal_conv(x, w, b):
    k = w.shape[0]
    y = lax.conv_general_dilated(
        x, w[:, None, :].astype(x.dtype), window_strides=(1,), padding=[(k - 1, 0)],
        dimension_numbers=('NWC', 'WIO', 'NWC'), feature_group_count=x.shape[-1])
    return y + b.astype(x.dtype)


RW_C = 64
RW_LANES = 256
RW_UNROLL = 4


def _dot(a, b):
    return jnp.dot(a.astype(BF16), b.astype(BF16), preferred_element_type=F32)


def _dot_nt(a, b):
    return lax.dot_general(a.astype(BF16), b.astype(BF16), (((1,), (1,)), ((), ())),
                           preferred_element_type=F32)


def _rwkv_kernel(r_ref, lw_ref, k_ref, v_ref, kk_ref, a_ref, y_ref,
                 st_ref, rm_ref, yn_ref, gc_ref, *, nc):
    c_len, n = RW_C, N_A
    hp = RW_LANES // n

    @pl.when(pl.program_id(2) == 0)
    def _():
        st_ref[...] = jnp.zeros_like(st_ref)

    row = lax.broadcasted_iota(jnp.int32, (c_len, c_len), 0)
    col = lax.broadcasted_iota(jnp.int32, (c_len, c_len), 1)
    tri_incl = (row >= col).astype(BF16)
    eye = (row == col).astype(F32)
    row2 = lax.broadcasted_iota(jnp.int32, (2 * c_len, 2 * c_len), 0)
    col2 = lax.broadcasted_iota(jnp.int32, (2 * c_len, 2 * c_len), 1)
    rr = jnp.where(row2 >= c_len, row2 - c_len, row2)
    cc = jnp.where(col2 >= c_len, col2 - c_len, col2)
    gmask = (rr - cc) >= jnp.where(row2 >= c_len, 0, 1)

    def prep(c):
        t0 = pl.multiple_of(c * c_len, c_len)
        sl = pl.ds(t0, c_len)
        lw = lw_ref[0, sl, :]
        r = r_ref[0, sl, :]
        k = k_ref[0, sl, :]
        v = v_ref[0, sl, :]
        kk = kk_ref[0, sl, :]
        a = a_ref[0, sl, :]
        lw_hi = lw.astype(BF16)
        lw_lo = (lw - lw_hi.astype(F32)).astype(BF16)
        cum2 = jnp.dot(tri_incl, jnp.concatenate([lw_hi, lw_lo], axis=1), preferred_element_type=F32)
        cum = cum2[:, :RW_LANES] + cum2[:, RW_LANES:]
        cum_prev = cum - lw
        cum_end = cum[c_len - 1:c_len, :]
        g_inv = jnp.exp(-cum)
        to_end = jnp.exp(cum_end - cum)
        beta = kk * a
        al_h = -kk * jnp.exp(cum_prev)
        r_h = r * jnp.exp(cum)
        be_c = beta * g_inv
        k_c = k * g_inv
        be_t = (beta * to_end).T
        k_t = (k * to_end).T
        gc_ref[c] = jnp.exp(cum.T[:, c_len - 1:c_len])
        probs = []
        for h in range(hp):
            hs = slice(h * n, (h + 1) * n)
            probs.append(dict(c=c, h=h, ah=al_h[:, hs], rh=r_h[:, hs], vh=v[:, hs],
                              bc=be_c[:, hs], kc=k_c[:, hs], bt=be_t[hs, :], kt=k_t[hs, :]))
        return probs

    def phase1(i, carry):
        probs = []
        for u in range(RW_UNROLL):
            probs += prep(i * RW_UNROLL + u)
        for q in probs:
            g = _dot_nt(jnp.concatenate([q["ah"], q["rh"]], axis=0),
                        jnp.concatenate([q["bc"], q["kc"]], axis=0))
            g = jnp.where(gmask, g, 0.0)
            q["a_ak"] = g[:c_len, c_len:]
            q["lower"] = g[c_len:, :]
            q["p"] = g[:c_len, :c_len]
            q["t"] = eye + q["p"]
        for _ in range(5):
            for q in probs:
                q["p"] = _dot(q["p"], q["p"])
            for q in probs:
                q["t"] = q["t"] + _dot(q["t"], q["p"])
        for q in probs:
            q["av"] = _dot(q["a_ak"], q["vh"])
        for q in probs:
            q["aw"] = _dot(q["t"], jnp.concatenate([q["ah"], q["av"]], axis=1))
        for q in probs:
            z = jnp.concatenate(
                [q["aw"], jnp.concatenate([jnp.zeros((c_len, n), F32), q["vh"]], axis=1)], axis=0)
            lhs = jnp.concatenate(
                [q["lower"], jnp.concatenate([q["bt"], q["kt"]], axis=1)], axis=0)
            f = _dot(lhs, z)
            rm = f[:, :n] + jnp.concatenate([q["rh"], jnp.zeros((n, n), F32)], axis=0)
            rm_ref[q["c"], q["h"]] = rm.astype(BF16)
            yn_ref[q["c"], q["h"]] = f[:, n:]
        return carry

    lax.fori_loop(0, nc // RW_UNROLL, phase1, 0)

    def phase2(c, carry):
        t0 = pl.multiple_of(c * c_len, c_len)
        gcol = gc_ref[c]
        ss = [st_ref[h] for h in range(hp)]
        fs = [jnp.dot(rm_ref[c, h], ss[h].astype(BF16), preferred_element_type=F32)
              for h in range(hp)]
        ys = []
        for h in range(hp):
            yn = yn_ref[c, h]
            ys.append(fs[h][:c_len] + yn[:c_len])
            st_ref[h] = gcol[h * n:(h + 1) * n] * ss[h] + fs[h][c_len:] + yn[c_len:]
        y_ref[0, pl.ds(t0, c_len), :] = jnp.concatenate(ys, axis=1)
        return carry

    lax.fori_loop(0, nc, phase2, 0)


def rwkv7_chunked(r, logw, k, v, kk, a, tb=512):
    bsz, seq, d = r.shape
    nc = tb // RW_C
    hp = RW_LANES // N_A
    spec = pl.BlockSpec((1, tb, RW_LANES), lambda b, h, t: (b, t, h))
    return pl.pallas_call(
        functools.partial(_rwkv_kernel, nc=nc),
        out_shape=jax.ShapeDtypeStruct((bsz, seq, d), F32),
        grid=(bsz, d // RW_LANES, seq // tb),
        in_specs=[spec] * 6,
        out_specs=spec,
        scratch_shapes=[
            pltpu.VMEM((hp, N_A, N_A), F32),
            pltpu.VMEM((nc, hp, RW_C + N_A, N_A), BF16),
            pltpu.VMEM((nc, hp, RW_C + N_A, N_A), F32),
            pltpu.VMEM((nc, RW_LANES, 1), F32),
        ],
        compiler_params=pltpu.CompilerParams(
            dimension_semantics=("parallel", "parallel", "arbitrary"),
            vmem_limit_bytes=VMEM_LIMIT_BYTES),
        name="rwkv7_chunked",
    )(r, logw, k, v, kk, a)


def rwkv7_heads(rkv, wa, dg, mu, w_up, a_up, g_up, vecs, r_k, v_first, vres):
    bsz, seq, _ = rkv.shape
    mix = lambda t, m: t + m * (token_shift(t) - t)
    rkv = mix(rkv, mu[:3 * D_A])
    wa = mix(wa, mu[3 * D_A:3 * D_A + R_W + R_A])
    dg = mix(dg, mu[3 * D_A + R_W + R_A:])
    r, k, v = rkv[..., :D_A], rkv[..., D_A:2 * D_A], rkv[..., 2 * D_A:]
    dw, da = wa[..., :R_W], wa[..., R_W:]
    w0, a0, k_k, k_a, lnx_w, lnx_b = (vecs[i] for i in range(6))
    w_log = -jax.nn.softplus(-(w0 + jnp.tanh(dw) @ w_up)) - 0.5
    log_decay = -jnp.exp(w_log)
    a = jax.nn.sigmoid(a0 + da @ a_up)
    g = jax.nn.sigmoid(dg) @ g_up
    if vres is None:
        v_first = v
    else:
        dv, mu_v, v_up, v0 = vres
        dv = mix(dv, mu_v)
        v = v + (v_first - v) * jax.nn.sigmoid(v0 + dv @ v_up)
    heads = lambda t: t.reshape(bsz, seq, H_A, N_A)
    kk = heads(k * k_k)
    kk = kk / jnp.maximum(jnp.linalg.norm(kk, axis=-1, keepdims=True), 1e-12)
    k = k * (1.0 + (a - 1.0) * k_a)
    rh, kh, vh = heads(r), heads(k), heads(v)
    y = heads(rwkv7_chunked(r, log_decay, k, v, kk.reshape(bsz, seq, D_A), a))
    mean = jnp.mean(y, axis=-1, keepdims=True)
    var = jnp.mean(jnp.square(y - mean), axis=-1, keepdims=True)
    y = ((y - mean) * lax.rsqrt(var + RWKV_GN_EPS)).reshape(bsz, seq, D_A) * lnx_w + lnx_b
    bonus = jnp.sum(rh * kh * r_k, axis=-1, keepdims=True) * vh
    y = (y + bonus.reshape(bsz, seq, D_A)) * g
    return y, v_first


def ssd_chunked(x, dt, a_neg, bm, cm):
    bsz, seq = x.shape[:2]
    nc = seq // CHUNK

    def chunks(t):
        return jnp.moveaxis(t.reshape((bsz, nc, CHUNK) + t.shape[2:]), 1, 0)

    causal = jnp.tril(jnp.ones((CHUNK, CHUNK), bool))[None, :, :, None, None]

    def step(state, inp):
        x_c, la_c, b_c, c_c = inp
        cum = jnp.cumsum(la_c, axis=1)
        seg = cum[:, :, None] - cum[:, None, :]
        decay = jnp.exp(jnp.where(causal, seg, -jnp.inf))
        scores = jnp.einsum('btgn,bsgn->btsg', c_c, b_c)[..., None] * decay
        y = jnp.einsum('btsgj,bsgjp->btgjp', scores, x_c)
        y = y + jnp.einsum('btgn,bgjpn->btgjp', c_c, state) * jnp.exp(cum)[..., None]
        decay_end = jnp.exp(cum[:, -1:] - cum)
        state = (state * jnp.exp(cum[:, -1])[..., None, None]
                 + jnp.einsum('bsgn,bsgj,bsgjp->bgjpn', b_c, decay_end, x_c))
        return state, y

    xs = (chunks(x.astype(F32) * dt[..., None]), chunks(dt * a_neg),
          chunks(bm.astype(F32)), chunks(cm.astype(F32)))
    s0 = jnp.zeros((bsz, G_B, J_B, P_B, N_B), F32)
    _, ys = lax.scan(step, s0, xs)
    return jnp.moveaxis(ys, 0, 1).reshape(x.shape)


def mamba2_heads(z, xbc, dt, conv_w, conv_b, head, norm_g):
    bsz, seq, _ = z.shape
    xbc = jax.nn.silu(causal_conv(xbc, conv_w, conv_b))
    xs, bm, cm = jnp.split(xbc, [D_B, D_B + G_B * N_B], axis=-1)
    dt_bias, a_log, d_skip = head[0], head[1], head[2]
    dt = jax.nn.softplus(dt.astype(F32) + dt_bias)
    a_neg = -jnp.exp(a_log.astype(F32)).reshape(G_B, J_B)
    xh = xs.reshape(bsz, seq, G_B, J_B, P_B).astype(F32)
    y = ssd_chunked(xh, dt.reshape(bsz, seq, G_B, J_B), a_neg,
                    bm.reshape(bsz, seq, G_B, N_B), cm.reshape(bsz, seq, G_B, N_B))
    y = y + d_skip.reshape(G_B, J_B)[..., None] * xh
    y = y.reshape(bsz, seq, D_B) * jax.nn.silu(z.astype(F32))
    yg = y.reshape(bsz, seq, G_B, D_B // G_B)
    yg = yg * lax.rsqrt(jnp.mean(yg * yg, axis=-1, keepdims=True) + MAMBA_NORM_EPS)
    return yg.reshape(bsz, seq, D_B) * norm_g


HY_GROUPS = (
    ("rkv", 0, 3 * D_A, 3 * D_A),
    ("wa", 3 * D_A, R_W + R_A, 128),
    ("g", 3 * D_A + R_W + R_A, R_G, 256),
    ("z", C_A, D_B, D_B),
    ("xbc", C_A + D_B, C_CONV, C_CONV),
    ("dt", C_A + D_B + C_CONV, H_B, 64),
    ("dv", C_HY, R_V, 64),
)


def hybrid_mixer(h, w_in, mu, w_up, a_up, g_up, vecs, r_k, conv_w, conv_b, head, mb_norm,
                 w_out, v_first, vres_params):
    bsz, seq, d = h.shape
    w_pad, offs = [], {}
    col = 0
    for name, start, width, padded in HY_GROUPS:
        width = min(width, max(w_in.shape[1] - start, 0))
        w_pad += [w_in[:, start:start + width], jnp.zeros((d, padded - width), w_in.dtype)]
        offs[name] = (col, width)
        col += padded
    p = matmul(h.reshape(bsz * seq, d), jnp.concatenate(w_pad, axis=1)).reshape(bsz, seq, -1)
    take = lambda name: p[..., offs[name][0]:offs[name][0] + offs[name][1]]
    vres = None if vres_params is None else (take("dv"),) + vres_params
    ya, v_first = rwkv7_heads(take("rkv"), take("wa"), take("g"), mu, w_up, a_up, g_up, vecs, r_k,
                              v_first, vres)
    yb = mamba2_heads(take("z"), take("xbc"), take("dt"), conv_w, conv_b, head, mb_norm)
    y = jnp.concatenate([ya, yb], axis=-1).astype(h.dtype)
    y = matmul(y.reshape(bsz * seq, D_CAT), w_out).reshape(bsz, seq, d)
    return y.astype(h.dtype), v_first


def linear_scan(a, b):
    def combine(left, right):
        a_l, b_l = left
        a_r, b_r = right
        return a_l * a_r, a_r * b_l + b_r
    _, hs = lax.associative_scan(combine, (a, b), axis=1)
    return hs


def rglru_mixer(h, w_in, conv_w, conv_b, gate_w, gate_b, lam, w_out):
    bsz, seq, d = h.shape
    p = matmul(h.reshape(bsz * seq, d), w_in).reshape(bsz, seq, -1)
    gate, xr = jnp.split(p, 2, axis=-1)
    xr = causal_conv(xr, conv_w, conv_b)
    xb = xr.reshape(bsz, seq, H_C, BS_C).astype(F32)
    gates = jax.nn.sigmoid(jnp.einsum('bshi,ghij->gbshj', xb, gate_w)
                           + gate_b.reshape(2, 1, 1, H_C, BS_C))
    r_gate, i_gate = gates[0], gates[1]
    log_a = -C_RG * r_gate * jax.nn.softplus(-lam.reshape(H_C, BS_C))
    a = jnp.exp(log_a)
    mult = jnp.sqrt(-jnp.expm1(2.0 * log_a))
    mult = jnp.where((jnp.arange(seq) == 0)[None, :, None, None], 1.0, mult)
    hs = linear_scan(a, mult * (i_gate * xb))
    y = hs.reshape(bsz, seq, D_RNN).astype(h.dtype) * jax.nn.gelu(gate)
    return matmul(y.reshape(bsz * seq, D_RNN), w_out).reshape(bsz, seq, d).astype(h.dtype)


def _moe_ffn_kernel(be_ref, x_ref, sw_ref, wgu_ref, wd_ref, o_ref):
    del be_ref
    gu = jnp.dot(x_ref[...], wgu_ref[0], preferred_element_type=F32)
    g, u = gu[:, :D_EXPERT], gu[:, D_EXPERT:]
    hb = (g * jax.nn.sigmoid(g)) * u
    y = jnp.dot(hb.astype(BF16), wd_ref[0], preferred_element_type=F32)
    o_ref[...] = y * sw_ref[...]


def moe_ffn(block_exp, xg, slot_w, w_gu, w_down):
    n_slots, d = xg.shape
    grid_spec = pltpu.PrefetchScalarGridSpec(
        num_scalar_prefetch=1,
        grid=(n_slots // MOE_BLOCK,),
        in_specs=[pl.BlockSpec((MOE_BLOCK, d), lambda i, be: (i, 0)),
                  pl.BlockSpec((MOE_BLOCK, 1), lambda i, be: (i, 0)),
                  pl.BlockSpec((1, d, 2 * D_EXPERT), lambda i, be: (be[i], 0, 0)),
                  pl.BlockSpec((1, D_EXPERT, d), lambda i, be: (be[i], 0, 0))],
        out_specs=pl.BlockSpec((MOE_BLOCK, d), lambda i, be: (i, 0)),
    )
    return pl.pallas_call(
        _moe_ffn_kernel,
        out_shape=jax.ShapeDtypeStruct((n_slots, d), F32),
        grid_spec=grid_spec,
        compiler_params=pltpu.CompilerParams(
            dimension_semantics=("arbitrary",), vmem_limit_bytes=VMEM_LIMIT_BYTES),
        name="moe_ffn",
    )(block_exp, xg, slot_w, w_gu, w_down)


def hier_moe(h, w_grp, b_grp, w_exp, b_exp, w_gate, w_up, w_down):
    bsz, seq, d = h.shape
    t = bsz * seq
    xt = h.reshape(t, d)
    logits = matmul(xt, jnp.concatenate([w_grp, w_exp], axis=1), tn=128)
    g_logits = logits[:, :N_GROUPS_E] + b_grp.astype(F32)
    g_idx = jnp.argmax(g_logits, axis=-1)
    g_w = jnp.take_along_axis(jax.nn.softmax(g_logits, axis=-1), g_idx[:, None], axis=-1)
    e_logits = (logits[:, N_GROUPS_E:] + b_exp.astype(F32)).reshape(t, N_GROUPS_E, E_PER_GROUP)
    e_logits = jnp.take_along_axis(e_logits, g_idx[:, None, None], axis=1)[:, 0]
    e_top, e_loc = lax.top_k(e_logits, TOP_K_E)
    e_w = jax.nn.softmax(e_top, axis=-1) * g_w
    e_flat = (g_idx[:, None] * E_PER_GROUP + e_loc).reshape(-1).astype(jnp.int32)
    w_flat = e_w.reshape(-1)
    tok_flat = jnp.repeat(jnp.arange(t, dtype=jnp.int32), TOP_K_E)
    n_assign = t * TOP_K_E
    order = jnp.argsort(e_flat)
    e_sorted = e_flat[order]
    counts = jnp.bincount(e_flat, length=N_EXPERTS)
    padded = (counts + MOE_BLOCK - 1) // MOE_BLOCK * MOE_BLOCK
    raw_start = jnp.cumsum(counts) - counts
    pad_start = jnp.cumsum(padded) - padded
    dest = pad_start[e_sorted] + (jnp.arange(n_assign, dtype=jnp.int32) - raw_start[e_sorted])
    n_slots = n_assign + N_EXPERTS * MOE_BLOCK
    n_blocks = n_slots // MOE_BLOCK
    slot_tok = jnp.full((n_slots,), t, jnp.int32).at[dest].set(tok_flat[order])
    slot_w = jnp.zeros((n_slots,), F32).at[dest].set(w_flat[order])
    block_start = jnp.arange(n_blocks, dtype=jnp.int32) * MOE_BLOCK
    block_exp = jnp.minimum(
        jnp.sum(block_start[:, None] >= (pad_start + padded)[None, :], axis=-1), N_EXPERTS - 1)
    x_pad = jnp.concatenate([xt.astype(BF16), jnp.zeros((1, d), BF16)], axis=0)
    y = moe_ffn(block_exp.astype(jnp.int32), x_pad[slot_tok], slot_w[:, None],
                jnp.concatenate([w_gate, w_up], axis=-1).astype(BF16), w_down.astype(BF16))
    slot_of = jnp.zeros((n_assign,), jnp.int32).at[order].set(dest.astype(jnp.int32))
    slot_of = slot_of.reshape(t, TOP_K_E)
    out = y[slot_of[:, 0]] + y[slot_of[:, 1]]
    return out.reshape(bsz, seq, d).astype(h.dtype)


def kernel(x, norm_mix, norm_ffn, norm_final, hy_w_in_first, hy_w_in_vres, rk_mu, rk_mu_v, rk_w_up, rk_a_up, rk_g_up, rk_vecs, rk_r_k, rk_v_up, rk_v0, mb_conv_w, mb_conv_b, mb_head, mb_norm, hy_w_out, lr_w_in, lr_conv_w, lr_conv_b, lr_gate_w, lr_gate_b, lr_lambda, lr_w_out, moe_w_grp, moe_b_grp, moe_w_exp, moe_b_exp, moe_w_gate, moe_w_up, moe_w_down):
    h = x
    v_first = None
    for layer in range(DEPTH):
        hn = rms_norm(h, norm_mix[layer])
        j = layer // 2
        if layer % 2 == 0:
            if j == 0:
                w_in, vres_params = hy_w_in_first, None
            else:
                w_in = hy_w_in_vres[j - 1]
                vres_params = (rk_mu_v[j - 1], rk_v_up[j - 1], rk_v0[j - 1])
            mix, v_first = hybrid_mixer(
                hn, w_in, rk_mu[j], rk_w_up[j], rk_a_up[j], rk_g_up[j], rk_vecs[j], rk_r_k[j],
                mb_conv_w[j], mb_conv_b[j], mb_head[j], mb_norm[j], hy_w_out[j], v_first, vres_params)
        else:
            mix = rglru_mixer(hn, lr_w_in[j], lr_conv_w[j], lr_conv_b[j], lr_gate_w[j],
                              lr_gate_b[j], lr_lambda[j], lr_w_out[j])
        h = h + mix
        h = h + hier_moe(rms_norm(h, norm_ffn[layer]), moe_w_grp[layer], moe_b_grp[layer],
                         moe_w_exp[layer], moe_b_exp[layer], moe_w_gate[layer],
                         moe_w_up[layer], moe_w_down[layer])
    return rms_norm(h, norm_final)
```

```python
import functools

import jax
import jax.numpy as jnp
from jax import lax
from jax.experimental import pallas as pl
from jax.experimental.pallas import tpu as pltpu

D_MODEL = 2048
DEPTH = 4
D_A = D_MODEL // 2
N_A = 64
H_A = D_A // N_A
R_W = 64
R_A = 64
R_V = 32
R_G = 160
RWKV_GN_EPS = 64e-5
C_A = 3 * D_A + R_W + R_A + R_G
D_B = D_MODEL
P_B = 64
H_B = D_B // P_B
N_B = 128
G_B = 4
J_B = H_B // G_B
CONV_K = 4
C_CONV = D_B + 2 * G_B * N_B
C_B = D_B + C_CONV + H_B
CHUNK = 128
MAMBA_NORM_EPS = 1e-5
C_HY = C_A + C_B
D_CAT = D_A + D_B
BS_C = 256
D_RNN = ((4 * D_MODEL // 3 + BS_C // 2) // BS_C) * BS_C
H_C = D_RNN // BS_C
C_RG = 8.0
N_GROUPS_E = 8
E_PER_GROUP = 8
N_EXPERTS = N_GROUPS_E * E_PER_GROUP
TOP_K_E = 2
D_EXPERT = D_MODEL // 8
MOE_BLOCK = 128
NORM_EPS = 1e-6
F32 = jnp.float32
BF16 = jnp.bfloat16

VMEM_LIMIT_BYTES = 56 * 1024 * 1024
LANES = 128
SUBLANES = 8
MM_TM = 512
MM_TN = 512
SEQ_TB = 512

OFF_Z = 0
OFF_XS = OFF_Z + D_B
OFF_RKV = OFF_XS + D_B
OFF_BM = OFF_RKV + 3 * D_A
OFF_CM = OFF_BM + G_B * N_B
OFF_G = OFF_CM + G_B * N_B
OFF_WA = OFF_G + 256
OFF_DT = OFF_WA + LANES
P_W = OFF_DT + LANES


def _block_index(offset, width):
    assert offset % width == 0, (offset, width)
    return offset // width


def _cparams(sem):
    return pltpu.CompilerParams(dimension_semantics=sem, vmem_limit_bytes=VMEM_LIMIT_BYTES)


def _dot(a, b):
    return jnp.dot(a.astype(BF16), b.astype(BF16), preferred_element_type=F32)


def _dot_nt(a, b):
    return lax.dot_general(a.astype(BF16), b.astype(BF16), (((1,), (1,)), ((), ())),
                           preferred_element_type=F32)


def _dot_hilo(x, w):
    hi = x.astype(BF16)
    lo = (x - hi.astype(F32)).astype(BF16)
    return (jnp.dot(hi, w, preferred_element_type=F32) + jnp.dot(lo, w, preferred_element_type=F32))


def _shift_rows(x, prev8, d):
    ext = jnp.concatenate([prev8, x], axis=0)
    return pltpu.roll(ext, d, axis=0)[SUBLANES:]


def _sigmoid(x):
    return 1.0 / (1.0 + jnp.exp(-x))


def _softplus(x):
    return jnp.maximum(x, 0.0) + jnp.log1p(jnp.exp(-jnp.abs(x)))


def _segment_ones(width, seg):
    shift = seg.bit_length() - 1
    r = lax.shift_right_logical(lax.broadcasted_iota(jnp.int32, (width, width), 0), shift)
    c = lax.shift_right_logical(lax.broadcasted_iota(jnp.int32, (width, width), 1), shift)
    return (r == c).astype(BF16)


def _norm_mm_kernel(x_ref, g_ref, w_ref, o_ref, xn_ref):
    @pl.when(pl.program_id(1) == 0)
    def _():
        x = x_ref[...]
        y = x * lax.rsqrt(jnp.mean(x * x, axis=-1, keepdims=True) + NORM_EPS) * g_ref[...]
        xn_ref[...] = y.astype(BF16)

    o_ref[...] = jnp.dot(xn_ref[...], w_ref[...], preferred_element_type=F32)


def norm_matmul(x, g, w):
    m, k = x.shape
    n = w.shape[1]
    return pl.pallas_call(
        _norm_mm_kernel,
        out_shape=jax.ShapeDtypeStruct((m, n), F32),
        grid=(m // MM_TM, n // MM_TN),
        in_specs=[pl.BlockSpec((MM_TM, k), lambda i, j: (i, 0)),
                  pl.BlockSpec((1, k), lambda i, j: (0, 0)),
                  pl.BlockSpec((k, MM_TN), lambda i, j: (0, j))],
        out_specs=pl.BlockSpec((MM_TM, MM_TN), lambda i, j: (i, j)),
        scratch_shapes=[pltpu.VMEM((MM_TM, k), BF16)],
        compiler_params=_cparams(("parallel", "arbitrary")),
        name="norm_matmul",
    )(x, g.reshape(1, k), w)


def _mm_res_kernel(*refs):
    *xw, res_ref, o_ref = refs
    acc = res_ref[...]
    for x_ref, w_ref in zip(xw[0::2], xw[1::2]):
        acc = acc + jnp.dot(x_ref[...], w_ref[...], preferred_element_type=F32)
    o_ref[...] = acc


def matmul_residual(xs, ws, res):
    m, n = res.shape
    in_specs, args = [], []
    for x, w in zip(xs, ws):
        in_specs += [pl.BlockSpec((MM_TM, x.shape[1]), lambda i, j: (i, 0)),
                     pl.BlockSpec((w.shape[0], MM_TN), lambda i, j: (0, j))]
        args += [x, w]
    in_specs.append(pl.BlockSpec((MM_TM, MM_TN), lambda i, j: (i, j)))
    return pl.pallas_call(
        _mm_res_kernel,
        out_shape=jax.ShapeDtypeStruct((m, n), F32),
        grid=(m // MM_TM, n // MM_TN),
        in_specs=in_specs,
        out_specs=pl.BlockSpec((MM_TM, MM_TN), lambda i, j: (i, j)),
        compiler_params=_cparams(("parallel", "arbitrary")),
        name="matmul_residual",
    )(*args, res)


def _norm_router_kernel(x_ref, g_ref, w_ref, xn_ref, lg_ref):
    x = x_ref[...]
    y = (x * lax.rsqrt(jnp.mean(x * x, axis=-1, keepdims=True) + NORM_EPS) * g_ref[...]).astype(BF16)
    xn_ref[...] = y
    lg_ref[...] = jnp.dot(y, w_ref[...], preferred_element_type=F32)


def norm_router(x, g, w):
    m, k = x.shape
    return pl.pallas_call(
        _norm_router_kernel,
        out_shape=(jax.ShapeDtypeStruct((m, k), BF16), jax.ShapeDtypeStruct((m, LANES), F32)),
        grid=(m // MM_TM,),
        in_specs=[pl.BlockSpec((MM_TM, k), lambda i: (i, 0)),
                  pl.BlockSpec((1, k), lambda i: (0, 0)),
                  pl.BlockSpec((k, LANES), lambda i: (0, 0))],
        out_specs=(pl.BlockSpec((MM_TM, k), lambda i: (i, 0)),
                   pl.BlockSpec((MM_TM, LANES), lambda i: (i, 0))),
        compiler_params=_cparams(("parallel",)),
        name="norm_router",
    )(x, g.reshape(1, k), w)


def _final_norm_kernel(x_ref, g_ref, o_ref):
    x = x_ref[...]
    o_ref[...] = x * lax.rsqrt(jnp.mean(x * x, axis=-1, keepdims=True) + NORM_EPS) * g_ref[...]


def final_norm(x, g):
    m, k = x.shape
    return pl.pallas_call(
        _final_norm_kernel,
        out_shape=jax.ShapeDtypeStruct((m, k), F32),
        grid=(m // MM_TM,),
        in_specs=[pl.BlockSpec((MM_TM, k), lambda i: (i, 0)), pl.BlockSpec((1, k), lambda i: (0, 0))],
        out_specs=pl.BlockSpec((MM_TM, k), lambda i: (i, 0)),
        compiler_params=_cparams(("parallel",)),
        name="final_norm",
    )(x, g.reshape(1, k))


RW_C = 64
RW_LANES = 256
RW_UNROLL = 4
RW_PREP = 128
(RV_MU_R, RV_MU_K, RV_MU_V, RV_W0, RV_A0, RV_KK, RV_KA, RV_LNW, RV_LNB, RV_RK, RV_V0) = range(11)
RV_ROWS = 16


def _rwkv_kernel(*refs, nc, vres):
    if vres:
        (pr_ref, pk_ref, pv_ref, pg_ref, pwa_ref, pdv_ref, vf_ref, vec_ref, muwa_ref, mug_ref, mudv_ref,
         wup_ref, aup_ref, gup_ref, vup_ref, ya_ref,
         st_ref, rm_ref, yn_ref, gc_ref, c_r, c_k, c_v, c_g, c_wa, c_dv,
         s_r, s_lw, s_k, s_v, s_kk, s_a, s_gate, s_bonus, s_y) = refs
        vfo_ref = None
    else:
        (pr_ref, pk_ref, pv_ref, pg_ref, pwa_ref, vec_ref, muwa_ref, mug_ref,
         wup_ref, aup_ref, gup_ref, ya_ref, vfo_ref,
         st_ref, rm_ref, yn_ref, gc_ref, c_r, c_k, c_v, c_g, c_wa,
         s_r, s_lw, s_k, s_v, s_kk, s_a, s_gate, s_bonus, s_y) = refs
    c_len, n = RW_C, N_A
    hp = RW_LANES // n
    tb = nc * c_len
    raws = [(pr_ref, c_r), (pk_ref, c_k), (pv_ref, c_v), (pg_ref, c_g), (pwa_ref, c_wa)]
    if vres:
        raws.append((pdv_ref, c_dv))

    @pl.when(pl.program_id(2) == 0)
    def _():
        st_ref[...] = jnp.zeros_like(st_ref)
        for _, c_ref in raws:
            c_ref[...] = jnp.zeros_like(c_ref)

    seg_ones = _segment_ones(RW_LANES, n)
    vec = vec_ref[...]
    row = lambda i: vec[i:i + 1, :]

    def prep0(i, carry):
        t0 = pl.multiple_of(i * RW_PREP, RW_PREP)
        sl = pl.ds(t0, RW_PREP)
        tp = pl.multiple_of(jnp.maximum(t0 - SUBLANES, 0), SUBLANES)

        def mixed(p_ref, c_ref, mu):
            x = p_ref[0, sl, :]
            prev8 = jnp.where(i == 0, c_ref[...], p_ref[0, pl.ds(tp, SUBLANES), :])
            return x + mu * (_shift_rows(x, prev8, 1) - x)

        r = mixed(pr_ref, c_r, row(RV_MU_R))
        k = mixed(pk_ref, c_k, row(RV_MU_K))
        v = mixed(pv_ref, c_v, row(RV_MU_V))
        dg = mixed(pg_ref, c_g, mug_ref[...])
        wa = mixed(pwa_ref, c_wa, muwa_ref[...])
        w_log = -_softplus(-(row(RV_W0) + _dot(jnp.tanh(wa), wup_ref[...]))) - 0.5
        s_lw[sl, :] = -jnp.exp(w_log)
        a = _sigmoid(row(RV_A0) + _dot(wa, aup_ref[...]))
        s_a[sl, :] = a
        s_gate[sl, :] = _dot(_sigmoid(dg), gup_ref[...])
        if vres:
            dv = mixed(pdv_ref, c_dv, mudv_ref[...])
            v = v + (vf_ref[0, sl, :] - v) * _sigmoid(row(RV_V0) + _dot(dv, vup_ref[...]))
        else:
            vfo_ref[0, sl, :] = v
        kk = k * row(RV_KK)
        norm = jnp.sqrt(_dot_hilo(kk * kk, seg_ones))
        s_kk[sl, :] = kk / jnp.maximum(norm, 1e-12)
        k = k * (1.0 + (a - 1.0) * row(RV_KA))
        s_r[sl, :] = r
        s_k[sl, :] = k
        s_v[sl, :] = v
        s_bonus[sl, :] = _dot_hilo(r * k * row(RV_RK), seg_ones) * v
        return carry

    lax.fori_loop(0, tb // RW_PREP, prep0, 0)
    for p_ref, c_ref in raws:
        c_ref[...] = p_ref[0, tb - SUBLANES:tb, :]

    rowi = lax.broadcasted_iota(jnp.int32, (c_len, c_len), 0)
    coli = lax.broadcasted_iota(jnp.int32, (c_len, c_len), 1)
    tri_incl = (rowi >= coli).astype(BF16)
    eye = (rowi == coli).astype(F32)
    row2 = lax.broadcasted_iota(jnp.int32, (2 * c_len, 2 * c_len), 0)
    col2 = lax.broadcasted_iota(jnp.int32, (2 * c_len, 2 * c_len), 1)
    rr = jnp.where(row2 >= c_len, row2 - c_len, row2)
    cc = jnp.where(col2 >= c_len, col2 - c_len, col2)
    gmask = (rr - cc) >= jnp.where(row2 >= c_len, 0, 1)

    def prep(c):
        t0 = pl.multiple_of(c * c_len, c_len)
        sl = pl.ds(t0, c_len)
        lw = s_lw[sl, :]
        r = s_r[sl, :]
        k = s_k[sl, :]
        v = s_v[sl, :]
        kk = s_kk[sl, :]
        a = s_a[sl, :]
        lw_hi = lw.astype(BF16)
        lw_lo = (lw - lw_hi.astype(F32)).astype(BF16)
        cum2 = jnp.dot(tri_incl, jnp.concatenate([lw_hi, lw_lo], axis=1), preferred_element_type=F32)
        cum = cum2[:, :RW_LANES] + cum2[:, RW_LANES:]
        cum_prev = cum - lw
        cum_end = cum[c_len - 1:c_len, :]
        g_inv = jnp.exp(-cum)
        to_end = jnp.exp(cum_end - cum)
        beta = kk * a
        al_h = -kk * jnp.exp(cum_prev)
        r_h = r * jnp.exp(cum)
        be_c = beta * g_inv
        k_c = k * g_inv
        be_t = (beta * to_end).T
        k_t = (k * to_end).T
        gc_ref[c] = jnp.exp(cum.T[:, c_len - 1:c_len])
        probs = []
        for h in range(hp):
            hs = slice(h * n, (h + 1) * n)
            probs.append(dict(c=c, h=h, ah=al_h[:, hs], rh=r_h[:, hs], vh=v[:, hs],
                              bc=be_c[:, hs], kc=k_c[:, hs], bt=be_t[hs, :], kt=k_t[hs, :]))
        return probs

    def phase1(i, carry):
        probs = []
        for u in range(RW_UNROLL):
            probs += prep(i * RW_UNROLL + u)
        for q in probs:
            g = _dot_nt(jnp.concatenate([q["ah"], q["rh"]], axis=0),
                        jnp.concatenate([q["bc"], q["kc"]], axis=0))
            g = jnp.where(gmask, g, 0.0)
            q["a_ak"] = g[:c_len, c_len:]
            q["lower"] = g[c_len:, :]
            q["p"] = g[:c_len, :c_len]
            q["t"] = eye + q["p"]
        for _ in range(5):
            for q in probs:
                q["p"] = _dot(q["p"], q["p"])
            for q in probs:
                q["t"] = q["t"] + _dot(q["t"], q["p"])
        for q in probs:
            q["av"] = _dot(q["a_ak"], q["vh"])
        for q in probs:
            q["aw"] = _dot(q["t"], jnp.concatenate([q["ah"], q["av"]], axis=1))
        for q in probs:
            z = jnp.concatenate(
                [q["aw"], jnp.concatenate([jnp.zeros((c_len, n), F32), q["vh"]], axis=1)], axis=0)
            lhs = jnp.concatenate(
                [q["lower"], jnp.concatenate([q["bt"], q["kt"]], axis=1)], axis=0)
            f = _dot(lhs, z)
            rm = f[:, :n] + jnp.concatenate([q["rh"], jnp.zeros((n, n), F32)], axis=0)
            rm_ref[q["c"], q["h"]] = rm.astype(BF16)
            yn_ref[q["c"], q["h"]] = f[:, n:]
        return carry

    lax.fori_loop(0, nc // RW_UNROLL, phase1, 0)

    def phase2(c, carry):
        t0 = pl.multiple_of(c * c_len, c_len)
        gcol = gc_ref[c]
        ss = [st_ref[h] for h in range(hp)]
        fs = [jnp.dot(rm_ref[c, h], ss[h].astype(BF16), preferred_element_type=F32)
              for h in range(hp)]
        ys = []
        for h in range(hp):
            yn = yn_ref[c, h]
            ys.append(fs[h][:c_len] + yn[:c_len])
            st_ref[h] = gcol[h * n:(h + 1) * n] * ss[h] + fs[h][c_len:] + yn[c_len:]
        s_y[pl.ds(t0, c_len), :] = jnp.concatenate(ys, axis=1)
        return carry

    lax.fori_loop(0, nc, phase2, 0)

    def post(i, carry):
        t0 = pl.multiple_of(i * RW_PREP, RW_PREP)
        sl = pl.ds(t0, RW_PREP)
        y = s_y[sl, :]
        mean = _dot_hilo(y, seg_ones) * (1.0 / n)
        yc = y - mean
        var = _dot_hilo(yc * yc, seg_ones) * (1.0 / n)
        yn = yc * lax.rsqrt(var + RWKV_GN_EPS) * row(RV_LNW) + row(RV_LNB)
        ya_ref[0, sl, :] = ((yn + s_bonus[sl, :]) * s_gate[sl, :]).astype(BF16)
        return carry

    lax.fori_loop(0, tb // RW_PREP, post, 0)


def rwkv7_mixer(p, mu, w_up, a_up, g_up, vecs, r_k, v_first, vres_params):
    bsz, seq, _ = p.shape
    vres = vres_params is not None
    tb = SEQ_TB
    nc = tb // RW_C
    hp = RW_LANES // N_A
    pad_rows = lambda w, before, total: jnp.pad(w, ((before, total - before - w.shape[0]), (0, 0))).astype(BF16)
    rows = [mu[:D_A], mu[D_A:2 * D_A], mu[2 * D_A:3 * D_A]] + [vecs[i] for i in range(6)] + [r_k.reshape(D_A)]
    rows.append(vres_params[2] if vres else jnp.zeros((D_A,), F32))
    table = jnp.pad(jnp.stack(rows), ((0, RV_ROWS - len(rows)), (0, 0)))
    mu_wa = mu[3 * D_A:3 * D_A + R_W + R_A].reshape(1, LANES)
    mu_g = jnp.pad(mu[3 * D_A + R_W + R_A:], (0, 256 - R_G)).reshape(1, 256)
    blk = lambda w, f: pl.BlockSpec((1, tb, w), f)
    col = lambda w: pl.BlockSpec((w.shape[0], RW_LANES), lambda b, h, t: (0, h))
    full = lambda w: pl.BlockSpec(w.shape, lambda b, h, t: (0, 0))
    q = RW_LANES
    wup = pad_rows(w_up, 0, LANES)
    aup = pad_rows(a_up, R_W, LANES)
    gup = pad_rows(g_up, 0, 256)
    in_specs = [blk(q, lambda b, h, t: (b, t, _block_index(OFF_RKV, q) + h)),
                blk(q, lambda b, h, t: (b, t, _block_index(OFF_RKV + D_A, q) + h)),
                blk(q, lambda b, h, t: (b, t, _block_index(OFF_RKV + 2 * D_A, q) + h)),
                blk(256, lambda b, h, t: (b, t, _block_index(OFF_G, 256))),
                blk(LANES, lambda b, h, t: (b, t, _block_index(OFF_WA, LANES)))]
    args = [p, p, p, p, p]
    if vres:
        mu_dv = jnp.pad(vres_params[0], (H_B, LANES - H_B - R_V)).reshape(1, LANES)
        vup = pad_rows(vres_params[1], H_B, LANES)
        in_specs += [blk(LANES, lambda b, h, t: (b, t, _block_index(OFF_DT, LANES))),
                     blk(q, lambda b, h, t: (b, t, h))]
        args += [p, v_first]
    in_specs += [col(table), full(mu_wa), full(mu_g)]
    args += [table, mu_wa, mu_g]
    if vres:
        in_specs.append(full(mu_dv))
        args.append(mu_dv)
    in_specs += [col(wup), col(aup), col(gup)]
    args += [wup, aup, gup]
    if vres:
        in_specs.append(col(vup))
        args.append(vup)
    out_blk = blk(q, lambda b, h, t: (b, t, h))
    out_shape = [jax.ShapeDtypeStruct((bsz, seq, D_A), BF16)]
    out_specs = [out_blk]
    if not vres:
        out_shape.append(jax.ShapeDtypeStruct((bsz, seq, D_A), F32))
        out_specs.append(out_blk)
    carries = [pltpu.VMEM((SUBLANES, w), F32) for w in (q, q, q, 256, LANES) + ((LANES,) if vres else ())]
    outs = pl.pallas_call(
        functools.partial(_rwkv_kernel, nc=nc, vres=vres),
        out_shape=tuple(out_shape),
        grid=(bsz, D_A // q, seq // tb),
        in_specs=in_specs,
        out_specs=tuple(out_specs),
        scratch_shapes=[
            pltpu.VMEM((hp, N_A, N_A), F32),
            pltpu.VMEM((nc, hp, RW_C + N_A, N_A), BF16),
            pltpu.VMEM((nc, hp, RW_C + N_A, N_A), F32),
            pltpu.VMEM((nc, q, 1), F32),
        ] + carries + [pltpu.VMEM((tb, q), F32)] * 9,
        compiler_params=_cparams(("parallel", "parallel", "arbitrary")),
        name="rwkv7_mixer",
    )(*args)
    return (outs[0], v_first) if vres else (outs[0], outs[1])


SSD_TB = 256


def _ssd_kernel(z_ref, xs_ref, bm_ref, cm_ref, dt_ref, cw_ref, cb_ref, hd_ref, dsk_ref, ng_ref, o_ref,
                st_ref, c_xs, c_bm, c_cm, s_xs, s_bm, s_cm, s_dt, s_la):
    tb = SSD_TB
    raws = [(xs_ref, c_xs, 0, D_B), (bm_ref, c_bm, D_B, G_B * N_B), (cm_ref, c_cm, D_B + G_B * N_B, G_B * N_B)]

    @pl.when(pl.program_id(1) == 0)
    def _():
        st_ref[...] = jnp.zeros_like(st_ref)
        for _, c_ref, _, _ in raws:
            c_ref[...] = jnp.zeros_like(c_ref)

    for (p_ref, c_ref, off, width), s_ref in zip(raws, (s_xs, s_bm, s_cm)):
        x = p_ref[0]
        prev8 = c_ref[...]
        acc = cb_ref[:, off:off + width] + cw_ref[CONV_K - 1:CONV_K, off:off + width] * x
        for d in range(1, CONV_K):
            acc = acc + cw_ref[CONV_K - 1 - d:CONV_K - d, off:off + width] * _shift_rows(x, prev8, d)
        s_ref[...] = acc * _sigmoid(acc)
        c_ref[...] = p_ref[0, tb - SUBLANES:tb, :]
    dt = _softplus(dt_ref[0] + hd_ref[0:1, :])
    s_dt[...] = dt
    s_la[...] = dt * (-jnp.exp(hd_ref[1:2, :]))

    rowi = lax.broadcasted_iota(jnp.int32, (CHUNK, CHUNK), 0)
    coli = lax.broadcasted_iota(jnp.int32, (CHUNK, CHUNK), 1)
    causal = rowi >= coli
    tri_incl = causal.astype(BF16)

    def chunk(c, carry):
        t0 = pl.multiple_of(c * CHUNK, CHUNK)
        sl = pl.ds(t0, CHUNK)
        la = s_la[sl, :]
        la_hi = la.astype(BF16)
        la_lo = (la - la_hi.astype(F32)).astype(BF16)
        cum2 = jnp.dot(tri_incl, jnp.concatenate([la_hi, la_lo], axis=1), preferred_element_type=F32)
        cum = cum2[:, :LANES] + cum2[:, LANES:]
        cum_t = cum.T
        ecum = jnp.exp(cum)
        dend = jnp.exp(cum[CHUNK - 1:CHUNK, :] - cum)
        dtc = s_dt[sl, :]
        outs = []
        for g in range(G_B):
            cmg = s_cm[sl, g * N_B:(g + 1) * N_B]
            bmg = s_bm[sl, g * N_B:(g + 1) * N_B]
            cb = _dot_nt(cmg, bmg)
            bmt = bmg.T
            hs = [g * J_B + j for j in range(J_B)]
            xdt = [s_xs[sl, h * P_B:(h + 1) * P_B] * dtc[:, h:h + 1] for h in hs]
            sts = [st_ref[h] for h in hs]
            sc = [cb * jnp.exp(jnp.where(causal, cum[:, h:h + 1] - cum_t[h:h + 1, :], -1e30)) for h in hs]
            y_in = [_dot(s, x) for s, x in zip(sc, xdt)]
            y_st = [_dot(cmg, st) for st in sts]
            upd = [_dot(bmt, x * dend[:, h:h + 1]) for x, h in zip(xdt, hs)]
            for j, h in enumerate(hs):
                st_ref[h] = sts[j] * ecum[CHUNK - 1:CHUNK, h:h + 1] + upd[j]
                outs.append(y_in[j] + y_st[j] * ecum[:, h:h + 1])
        y = jnp.concatenate(outs, axis=1)
        zc = z_ref[0, sl, :]
        y = (y + dsk_ref[...] * s_xs[sl, :]) * (zc * _sigmoid(zc))
        pieces = []
        gw = D_B // G_B
        for g in range(G_B):
            yg = y[:, g * gw:(g + 1) * gw]
            pieces.append(yg * lax.rsqrt(jnp.mean(yg * yg, axis=-1, keepdims=True) + MAMBA_NORM_EPS))
        o_ref[0, sl, :] = (jnp.concatenate(pieces, axis=1) * ng_ref[...]).astype(BF16)
        return carry

    lax.fori_loop(0, tb // CHUNK, chunk, 0)


def mamba2_mixer(p, conv_w, conv_b, head, norm_g):
    bsz, seq, _ = p.shape
    tb = SSD_TB
    hd = jnp.pad(head[:2], ((0, SUBLANES - 2), (0, LANES - H_B)))
    dsk = jnp.repeat(head[2], P_B).reshape(1, D_B)
    gw = G_B * N_B
    blk = lambda w, off: pl.BlockSpec((1, tb, w), lambda b, t: (b, t, _block_index(off, w)))
    full = lambda a: pl.BlockSpec(a.shape, lambda b, t: (0,) * a.ndim)
    cb = conv_b.reshape(1, C_CONV)
    ng = norm_g.reshape(1, D_B)
    return pl.pallas_call(
        _ssd_kernel,
        out_shape=jax.ShapeDtypeStruct((bsz, seq, D_B), BF16),
        grid=(bsz, seq // tb),
        in_specs=[blk(D_B, OFF_Z), blk(D_B, OFF_XS), blk(gw, OFF_BM), blk(gw, OFF_CM), blk(LANES, OFF_DT),
                  full(conv_w), full(cb), full(hd), full(dsk), full(ng)],
        out_specs=pl.BlockSpec((1, tb, D_B), lambda b, t: (b, t, 0)),
        scratch_shapes=[
            pltpu.VMEM((H_B, N_B, P_B), F32),
            pltpu.VMEM((SUBLANES, D_B), F32), pltpu.VMEM((SUBLANES, gw), F32), pltpu.VMEM((SUBLANES, gw), F32),
            pltpu.VMEM((tb, D_B), F32), pltpu.VMEM((tb, gw), F32), pltpu.VMEM((tb, gw), F32),
            pltpu.VMEM((tb, LANES), F32), pltpu.VMEM((tb, LANES), F32),
        ],
        compiler_params=_cparams(("parallel", "arbitrary")),
        name="mamba2_mixer",
    )(p, p, p, p, p, conv_w, cb, hd, dsk, ng)


def _lru_kernel(gate_ref, xr_ref, cw_ref, cb_ref, gw_ref, gb_ref, lam_ref, o_ref, c_x, h_ref):
    tb = SEQ_TB
    first = pl.program_id(2) == 0

    @pl.when(first)
    def _():
        c_x[...] = jnp.zeros_like(c_x)
        h_ref[...] = jnp.zeros_like(h_ref)

    x = xr_ref[0]
    prev8 = c_x[...]
    xb = cb_ref[...] + cw_ref[CONV_K - 1:CONV_K, :] * x
    for d in range(1, CONV_K):
        xb = xb + cw_ref[CONV_K - 1 - d:CONV_K - d, :] * _shift_rows(x, prev8, d)
    c_x[...] = xr_ref[0, tb - SUBLANES:tb, :]
    gates = _sigmoid(_dot(xb, gw_ref[0]) + gb_ref[0])
    r_gate, i_gate = gates[:, :BS_C], gates[:, BS_C:]
    log_a = -C_RG * r_gate * _softplus(-lam_ref[...])
    a = jnp.exp(log_a)
    mult = jnp.sqrt(-jnp.tanh(log_a) * (a * a + 1.0))
    rowi = lax.broadcasted_iota(jnp.int32, (tb, BS_C), 0)
    mult = jnp.where(jnp.logical_and(first, rowi == 0), 1.0, mult)
    b = mult * (i_gate * xb)
    d = 1
    while d < tb:
        if d < SUBLANES:
            a_s = jnp.where(rowi >= d, pltpu.roll(a, d, axis=0), 1.0)
            b_s = jnp.where(rowi >= d, pltpu.roll(b, d, axis=0), 0.0)
        else:
            a_s = jnp.concatenate([jnp.ones((d, BS_C), F32), a[:tb - d]], axis=0)
            b_s = jnp.concatenate([jnp.zeros((d, BS_C), F32), b[:tb - d]], axis=0)
        b = b + a * b_s
        a = a * a_s
        d *= 2
    h = b + a * h_ref[...]
    h_ref[...] = h[tb - 1:tb, :]
    g = gate_ref[0]
    gelu = 0.5 * g * (1.0 + jnp.tanh(0.7978845608028654 * (g + 0.044715 * (g * g * g))))
    o_ref[0] = (h * gelu).astype(BF16)


def rglru_mixer(p, conv_w, conv_b, gate_w, gate_b, lam):
    bsz, seq, _ = p.shape
    tb = SEQ_TB
    gw = jnp.concatenate([gate_w[0], gate_w[1]], axis=-1).astype(BF16)
    gb = jnp.concatenate([gate_b[0].reshape(H_C, 1, BS_C), gate_b[1].reshape(H_C, 1, BS_C)], axis=-1)
    vec = lambda a: pl.BlockSpec((a.shape[0], BS_C), lambda b, h, t: (0, h))
    cb = conv_b.reshape(1, D_RNN)
    lm = lam.reshape(1, D_RNN)
    return pl.pallas_call(
        _lru_kernel,
        out_shape=jax.ShapeDtypeStruct((bsz, seq, D_RNN), BF16),
        grid=(bsz, H_C, seq // tb),
        in_specs=[pl.BlockSpec((1, tb, BS_C), lambda b, h, t: (b, t, h)),
                  pl.BlockSpec((1, tb, BS_C), lambda b, h, t: (b, t, H_C + h)),
                  vec(conv_w), vec(cb),
                  pl.BlockSpec((1, BS_C, 2 * BS_C), lambda b, h, t: (h, 0, 0)),
                  pl.BlockSpec((1, 1, 2 * BS_C), lambda b, h, t: (h, 0, 0)),
                  vec(lm)],
        out_specs=pl.BlockSpec((1, tb, BS_C), lambda b, h, t: (b, t, h)),
        scratch_shapes=[pltpu.VMEM((SUBLANES, BS_C), F32), pltpu.VMEM((1, BS_C), F32)],
        compiler_params=_cparams(("parallel", "parallel", "arbitrary")),
        name="rglru_mixer",
    )(p, p, conv_w, cb, gw, gb, lm)


def _moe_ffn_kernel(be_ref, x_ref, wgu_ref, wd_ref, o_ref):
    del be_ref
    gu = jnp.dot(x_ref[...], wgu_ref[0], preferred_element_type=F32)
    g, u = gu[:, :D_EXPERT], gu[:, D_EXPERT:]
    hb = (g * jax.nn.sigmoid(g)) * u
    o_ref[...] = jnp.dot(hb.astype(BF16), wd_ref[0], preferred_element_type=F32)


def moe_ffn(block_exp, xg, w_gu, w_down):
    n_slots, d = xg.shape
    grid_spec = pltpu.PrefetchScalarGridSpec(
        num_scalar_prefetch=1,
        grid=(n_slots // MOE_BLOCK,),
        in_specs=[pl.BlockSpec((MOE_BLOCK, d), lambda i, be: (i, 0)),
                  pl.BlockSpec((1, d, 2 * D_EXPERT), lambda i, be: (be[i], 0, 0)),
                  pl.BlockSpec((1, D_EXPERT, d), lambda i, be: (be[i], 0, 0))],
        out_specs=pl.BlockSpec((MOE_BLOCK, d), lambda i, be: (i, 0)),
    )
    return pl.pallas_call(
        _moe_ffn_kernel,
        out_shape=jax.ShapeDtypeStruct((n_slots, d), F32),
        grid_spec=grid_spec,
        compiler_params=_cparams(("arbitrary",)),
        name="moe_ffn",
    )(block_exp, xg, w_gu, w_down)


def hier_moe(h, norm_g, w_grp, b_grp, w_exp, b_exp, w_gate, w_up, w_down):
    t, d = h.shape
    w_r = jnp.pad(jnp.concatenate([w_grp, w_exp], axis=1), ((0, 0), (0, LANES - N_GROUPS_E - N_EXPERTS)))
    xn, logits = norm_router(h, norm_g, w_r.astype(BF16))
    g_logits = logits[:, :N_GROUPS_E] + b_grp.astype(F32)
    g_idx = jnp.argmax(g_logits, axis=-1)
    g_w = jnp.take_along_axis(jax.nn.softmax(g_logits, axis=-1), g_idx[:, None], axis=-1)
    e_logits = (logits[:, N_GROUPS_E:N_GROUPS_E + N_EXPERTS] + b_exp.astype(F32)).reshape(
        t, N_GROUPS_E, E_PER_GROUP)
    e_logits = jnp.take_along_axis(e_logits, g_idx[:, None, None], axis=1)[:, 0]
    e_top, e_loc = lax.top_k(e_logits, TOP_K_E)
    e_w = jax.nn.softmax(e_top, axis=-1) * g_w
    e_flat = (g_idx[:, None] * E_PER_GROUP + e_loc).reshape(-1).astype(jnp.int32)
    tok_flat = jnp.repeat(jnp.arange(t, dtype=jnp.int32), TOP_K_E)
    n_assign = t * TOP_K_E
    onehot = (e_flat[:, None] == jnp.arange(N_EXPERTS, dtype=jnp.int32)[None, :]).astype(jnp.int32)
    running = jnp.cumsum(onehot, axis=0)
    counts = running[-1]
    rank = jnp.take_along_axis(running, e_flat[:, None], axis=1)[:, 0] - 1
    padded = (counts + MOE_BLOCK - 1) // MOE_BLOCK * MOE_BLOCK
    pad_start = jnp.cumsum(padded) - padded
    dest = pad_start[e_flat] + rank
    n_slots = n_assign + N_EXPERTS * MOE_BLOCK
    n_blocks = n_slots // MOE_BLOCK
    slot_tok = jnp.full((n_slots,), t, jnp.int32).at[dest].set(tok_flat)
    block_start = jnp.arange(n_blocks, dtype=jnp.int32) * MOE_BLOCK
    block_exp = jnp.minimum(
        jnp.sum(block_start[:, None] >= (pad_start + padded)[None, :], axis=-1), N_EXPERTS - 1)
    x_pad = jnp.concatenate([xn, jnp.zeros((1, d), BF16)], axis=0)
    y = moe_ffn(block_exp.astype(jnp.int32), x_pad[slot_tok],
                jnp.concatenate([w_gate, w_up], axis=-1).astype(BF16), w_down.astype(BF16))
    slot_of = dest.reshape(t, TOP_K_E)
    return h + (y[slot_of[:, 0]] * e_w[:, 0:1] + y[slot_of[:, 1]] * e_w[:, 1:2])


def _hybrid_w_in(w_in):
    d = w_in.shape[0]
    zeros = lambda n: jnp.zeros((d, n), w_in.dtype)
    c_wa, c_g = 3 * D_A, 3 * D_A + R_W + R_A
    dv = w_in[:, C_HY:C_HY + R_V] if w_in.shape[1] > C_HY else zeros(R_V)
    c_x = C_A + D_B
    return jnp.concatenate([
        w_in[:, C_A:c_x], w_in[:, c_x:c_x + D_B], w_in[:, :3 * D_A], w_in[:, c_x + D_B:c_x + C_CONV],
        w_in[:, c_g:C_A], zeros(256 - R_G), w_in[:, c_wa:c_g],
        w_in[:, c_x + C_CONV:C_HY], dv, zeros(LANES - H_B - R_V)], axis=1).astype(BF16)


def kernel(x, norm_mix, norm_ffn, norm_final, hy_w_in_first, hy_w_in_vres, rk_mu, rk_mu_v, rk_w_up, rk_a_up, rk_g_up, rk_vecs, rk_r_k, rk_v_up, rk_v0, mb_conv_w, mb_conv_b, mb_head, mb_norm, hy_w_out, lr_w_in, lr_conv_w, lr_conv_b, lr_gate_w, lr_gate_b, lr_lambda, lr_w_out, moe_w_grp, moe_b_grp, moe_w_exp, moe_b_exp, moe_w_gate, moe_w_up, moe_w_down):
    bsz, seq, d = x.shape
    t = bsz * seq
    h = x.reshape(t, d)
    v_first = None
    for layer in range(DEPTH):
        j = layer // 2
        if layer % 2 == 0:
            w_in = hy_w_in_first if j == 0 else hy_w_in_vres[j - 1]
            vres_params = None if j == 0 else (rk_mu_v[j - 1], rk_v_up[j - 1], rk_v0[j - 1])
            p = norm_matmul(h, norm_mix[layer], _hybrid_w_in(w_in)).reshape(bsz, seq, P_W)
            ya, v_first = rwkv7_mixer(p, rk_mu[j], rk_w_up[j], rk_a_up[j], rk_g_up[j], rk_vecs[j], rk_r_k[j],
                                      v_first, vres_params)
            yb = mamba2_mixer(p, mb_conv_w[j], mb_conv_b[j], mb_head[j], mb_norm[j])
            w_out = hy_w_out[j].astype(BF16)
            h = matmul_residual([ya.reshape(t, D_A), yb.reshape(t, D_B)], [w_out[:D_A], w_out[D_A:]], h)
        else:
            p = norm_matmul(h, norm_mix[layer], lr_w_in[j].astype(BF16)).reshape(bsz, seq, 2 * D_RNN)
            y = rglru_mixer(p, lr_conv_w[j], lr_conv_b[j], lr_gate_w[j], lr_gate_b[j], lr_lambda[j])
            h = matmul_residual([y.reshape(t, D_RNN)], [lr_w_out[j].astype(BF16)], h)
        h = hier_moe(h, norm_ffn[layer], moe_w_grp[layer], moe_b_grp[layer], moe_w_exp[layer],
                     moe_b_exp[layer], moe_w_gate[layer], moe_w_up[layer], moe_w_down[layer])
    return final_norm(h, norm_final).reshape(bsz, seq, d)
```

```python
import functools

import jax
import jax.numpy as jnp
from jax import lax
from jax.experimental import pallas as pl
from jax.experimental.pallas import tpu as pltpu

D_MODEL = 2048
DEPTH = 4
D_A = D_MODEL // 2
N_A = 64
H_A = D_A // N_A
R_W = 64
R_A = 64
R_V = 32
R_G = 160
RWKV_GN_EPS = 64e-5
C_A = 3 * D_A + R_W + R_A + R_G
D_B = D_MODEL
P_B = 64
H_B = D_B // P_B
N_B = 128
G_B = 4
J_B = H_B // G_B
CONV_K = 4
C_CONV = D_B + 2 * G_B * N_B
C_B = D_B + C_CONV + H_B
CHUNK = 128
MAMBA_NORM_EPS = 1e-5
C_HY = C_A + C_B
D_CAT = D_A + D_B
BS_C = 256
D_RNN = ((4 * D_MODEL // 3 + BS_C // 2) // BS_C) * BS_C
H_C = D_RNN // BS_C
C_RG = 8.0
N_GROUPS_E = 8
E_PER_GROUP = 8
N_EXPERTS = N_GROUPS_E * E_PER_GROUP
TOP_K_E = 2
D_EXPERT = D_MODEL // 8
MOE_BLOCK = 128
NORM_EPS = 1e-6
F32 = jnp.float32
BF16 = jnp.bfloat16

VMEM_LIMIT_BYTES = 56 * 1024 * 1024
LANES = 128
SUBLANES = 8
MM_TM = 1024
MM_TN = 512
SEQ_TB = 512

OFF_Z = 0
OFF_XS = OFF_Z + D_B
OFF_RKV = OFF_XS + D_B
OFF_BM = OFF_RKV + 3 * D_A
OFF_CM = OFF_BM + G_B * N_B
OFF_G = OFF_CM + G_B * N_B
OFF_WA = OFF_G + 256
OFF_DT = OFF_WA + LANES
P_W = OFF_DT + LANES


def _block_index(offset, width):
    assert offset % width == 0, (offset, width)
    return offset // width


def _cparams(sem):
    return pltpu.CompilerParams(dimension_semantics=sem, vmem_limit_bytes=VMEM_LIMIT_BYTES)


def _dot(a, b):
    return jnp.dot(a.astype(BF16), b.astype(BF16), preferred_element_type=F32)


def _dot_nt(a, b):
    return lax.dot_general(a.astype(BF16), b.astype(BF16), (((1,), (1,)), ((), ())),
                           preferred_element_type=F32)


def _dot_hilo(x, w):
    hi = x.astype(BF16)
    lo = (x - hi.astype(F32)).astype(BF16)
    return (jnp.dot(hi, w, preferred_element_type=F32) + jnp.dot(lo, w, preferred_element_type=F32))


def _shift_rows(x, prev8, d):
    ext = jnp.concatenate([prev8, x], axis=0)
    return pltpu.roll(ext, d, axis=0)[SUBLANES:]


def _sigmoid(x):
    return 1.0 / (1.0 + jnp.exp(-x))


def _softplus(x):
    return jnp.maximum(x, 0.0) + jnp.log1p(jnp.exp(-jnp.abs(x)))


def _segment_ones(width, seg):
    shift = seg.bit_length() - 1
    r = lax.shift_right_logical(lax.broadcasted_iota(jnp.int32, (width, width), 0), shift)
    c = lax.shift_right_logical(lax.broadcasted_iota(jnp.int32, (width, width), 1), shift)
    return (r == c).astype(BF16)


def _norm_mm_kernel(x_ref, g_ref, w_ref, o_ref, xn_ref):
    @pl.when(pl.program_id(1) == 0)
    def _():
        x = x_ref[...]
        y = x * lax.rsqrt(jnp.mean(x * x, axis=-1, keepdims=True) + NORM_EPS) * g_ref[...]
        xn_ref[...] = y.astype(BF16)

    o_ref[...] = jnp.dot(xn_ref[...], w_ref[...], preferred_element_type=F32)


def norm_matmul(x, g, w):
    m, k = x.shape
    n = w.shape[1]
    return pl.pallas_call(
        _norm_mm_kernel,
        out_shape=jax.ShapeDtypeStruct((m, n), F32),
        grid=(m // MM_TM, n // MM_TN),
        in_specs=[pl.BlockSpec((MM_TM, k), lambda i, j: (i, 0)),
                  pl.BlockSpec((1, k), lambda i, j: (0, 0)),
                  pl.BlockSpec((k, MM_TN), lambda i, j: (0, j))],
        out_specs=pl.BlockSpec((MM_TM, MM_TN), lambda i, j: (i, j)),
        scratch_shapes=[pltpu.VMEM((MM_TM, k), BF16)],
        compiler_params=_cparams(("parallel", "arbitrary")),
        name="norm_matmul",
    )(x, g.reshape(1, k), w)


def _mm_res_kernel(*refs):
    *xw, res_ref, o_ref = refs
    acc = res_ref[...]
    for x_ref, w_ref in zip(xw[0::2], xw[1::2]):
        acc = acc + jnp.dot(x_ref[...], w_ref[...], preferred_element_type=F32)
    o_ref[...] = acc


def matmul_residual(xs, ws, res):
    m, n = res.shape
    in_specs, args = [], []
    for x, w in zip(xs, ws):
        in_specs += [pl.BlockSpec((MM_TM, x.shape[1]), lambda i, j: (i, 0)),
                     pl.BlockSpec((w.shape[0], MM_TN), lambda i, j: (0, j))]
        args += [x, w]
    in_specs.append(pl.BlockSpec((MM_TM, MM_TN), lambda i, j: (i, j)))
    return pl.pallas_call(
        _mm_res_kernel,
        out_shape=jax.ShapeDtypeStruct((m, n), F32),
        grid=(m // MM_TM, n // MM_TN),
        in_specs=in_specs,
        out_specs=pl.BlockSpec((MM_TM, MM_TN), lambda i, j: (i, j)),
        compiler_params=_cparams(("parallel", "arbitrary")),
        name="matmul_residual",
    )(*args, res)


def _norm_router_kernel(x_ref, g_ref, w_ref, xn_ref, lg_ref):
    x = x_ref[...]
    y = (x * lax.rsqrt(jnp.mean(x * x, axis=-1, keepdims=True) + NORM_EPS) * g_ref[...]).astype(BF16)
    xn_ref[...] = y
    lg_ref[...] = jnp.dot(y, w_ref[...], preferred_element_type=F32)


def norm_router(x, g, w):
    m, k = x.shape
    return pl.pallas_call(
        _norm_router_kernel,
        out_shape=(jax.ShapeDtypeStruct((m, k), BF16), jax.ShapeDtypeStruct((m, LANES), F32)),
        grid=(m // MM_TM,),
        in_specs=[pl.BlockSpec((MM_TM, k), lambda i: (i, 0)),
                  pl.BlockSpec((1, k), lambda i: (0, 0)),
                  pl.BlockSpec((k, LANES), lambda i: (0, 0))],
        out_specs=(pl.BlockSpec((MM_TM, k), lambda i: (i, 0)),
                   pl.BlockSpec((MM_TM, LANES), lambda i: (i, 0))),
        compiler_params=_cparams(("parallel",)),
        name="norm_router",
    )(x, g.reshape(1, k), w)


def _final_norm_kernel(x_ref, g_ref, o_ref):
    x = x_ref[...]
    o_ref[...] = x * lax.rsqrt(jnp.mean(x * x, axis=-1, keepdims=True) + NORM_EPS) * g_ref[...]


def final_norm(x, g):
    m, k = x.shape
    return pl.pallas_call(
        _final_norm_kernel,
        out_shape=jax.ShapeDtypeStruct((m, k), F32),
        grid=(m // MM_TM,),
        in_specs=[pl.BlockSpec((MM_TM, k), lambda i: (i, 0)), pl.BlockSpec((1, k), lambda i: (0, 0))],
        out_specs=pl.BlockSpec((MM_TM, k), lambda i: (i, 0)),
        compiler_params=_cparams(("parallel",)),
        name="final_norm",
    )(x, g.reshape(1, k))


RW_C = 64
RW_LANES = 256
RW_UNROLL = 8
RW_PREP = 256
(RV_MU_R, RV_MU_K, RV_MU_V, RV_W0, RV_A0, RV_KK, RV_KA, RV_LNW, RV_LNB, RV_RK, RV_V0) = range(11)
RV_ROWS = 16


def _rwkv_kernel(*refs, nc, vres):
    if vres:
        (pr_ref, pk_ref, pv_ref, pg_ref, pwa_ref, pdv_ref, vf_ref, vec_ref, muwa_ref, mug_ref, mudv_ref,
         wup_ref, aup_ref, gup_ref, vup_ref, ya_ref,
         st_ref, rm_ref, yn_ref, gc_ref, c_r, c_k, c_v, c_g, c_wa, c_dv,
         s_r, s_lw, s_k, s_v, s_kk, s_a, s_gate, s_bonus, s_y) = refs
        vfo_ref = None
    else:
        (pr_ref, pk_ref, pv_ref, pg_ref, pwa_ref, vec_ref, muwa_ref, mug_ref,
         wup_ref, aup_ref, gup_ref, ya_ref, vfo_ref,
         st_ref, rm_ref, yn_ref, gc_ref, c_r, c_k, c_v, c_g, c_wa,
         s_r, s_lw, s_k, s_v, s_kk, s_a, s_gate, s_bonus, s_y) = refs
    c_len, n = RW_C, N_A
    hp = RW_LANES // n
    tb = nc * c_len
    raws = [(pr_ref, c_r), (pk_ref, c_k), (pv_ref, c_v), (pg_ref, c_g), (pwa_ref, c_wa)]
    if vres:
        raws.append((pdv_ref, c_dv))

    @pl.when(pl.program_id(2) == 0)
    def _():
        st_ref[...] = jnp.zeros_like(st_ref)
        for _, c_ref in raws:
            c_ref[...] = jnp.zeros_like(c_ref)

    seg_ones = _segment_ones(RW_LANES, n)
    vec = vec_ref[...]
    row = lambda i: vec[i:i + 1, :]

    def prep0(i, carry):
        t0 = pl.multiple_of(i * RW_PREP, RW_PREP)
        sl = pl.ds(t0, RW_PREP)
        tp = pl.multiple_of(jnp.maximum(t0 - SUBLANES, 0), SUBLANES)

        def mixed(p_ref, c_ref, mu):
            x = p_ref[0, sl, :]
            prev8 = jnp.where(i == 0, c_ref[...], p_ref[0, pl.ds(tp, SUBLANES), :])
            return x + mu * (_shift_rows(x, prev8, 1) - x)

        r = mixed(pr_ref, c_r, row(RV_MU_R))
        k = mixed(pk_ref, c_k, row(RV_MU_K))
        v = mixed(pv_ref, c_v, row(RV_MU_V))
        dg = mixed(pg_ref, c_g, mug_ref[...])
        wa = mixed(pwa_ref, c_wa, muwa_ref[...])
        w_log = -_softplus(-(row(RV_W0) + _dot(jnp.tanh(wa), wup_ref[...]))) - 0.5
        s_lw[sl, :] = -jnp.exp(w_log)
        a = _sigmoid(row(RV_A0) + _dot(wa, aup_ref[...]))
        s_a[sl, :] = a
        s_gate[sl, :] = _dot(_sigmoid(dg), gup_ref[...])
        if vres:
            dv = mixed(pdv_ref, c_dv, mudv_ref[...])
            v = v + (vf_ref[0, sl, :] - v) * _sigmoid(row(RV_V0) + _dot(dv, vup_ref[...]))
        else:
            vfo_ref[0, sl, :] = v
        kk = k * row(RV_KK)
        norm = jnp.sqrt(_dot_hilo(kk * kk, seg_ones))
        s_kk[sl, :] = kk / jnp.maximum(norm, 1e-12)
        k = k * (1.0 + (a - 1.0) * row(RV_KA))
        s_r[sl, :] = r
        s_k[sl, :] = k
        s_v[sl, :] = v
        s_bonus[sl, :] = _dot_hilo(r * k * row(RV_RK), seg_ones) * v
        return carry

    lax.fori_loop(0, tb // RW_PREP, prep0, 0)
    for p_ref, c_ref in raws:
        c_ref[...] = p_ref[0, tb - SUBLANES:tb, :]

    rowi = lax.broadcasted_iota(jnp.int32, (c_len, c_len), 0)
    coli = lax.broadcasted_iota(jnp.int32, (c_len, c_len), 1)
    tri_incl = (rowi >= coli).astype(BF16)
    eye = (rowi == coli).astype(F32)
    row2 = lax.broadcasted_iota(jnp.int32, (2 * c_len, 2 * c_len), 0)
    col2 = lax.broadcasted_iota(jnp.int32, (2 * c_len, 2 * c_len), 1)
    rr = jnp.where(row2 >= c_len, row2 - c_len, row2)
    cc = jnp.where(col2 >= c_len, col2 - c_len, col2)
    gmask = (rr - cc) >= jnp.where(row2 >= c_len, 0, 1)

    def prep(c):
        t0 = pl.multiple_of(c * c_len, c_len)
        sl = pl.ds(t0, c_len)
        lw = s_lw[sl, :]
        r = s_r[sl, :]
        k = s_k[sl, :]
        v = s_v[sl, :]
        kk = s_kk[sl, :]
        a = s_a[sl, :]
        lw_hi = lw.astype(BF16)
        lw_lo = (lw - lw_hi.astype(F32)).astype(BF16)
        cum2 = jnp.dot(tri_incl, jnp.concatenate([lw_hi, lw_lo], axis=1), preferred_element_type=F32)
        cum = cum2[:, :RW_LANES] + cum2[:, RW_LANES:]
        cum_prev = cum - lw
        cum_end = cum[c_len - 1:c_len, :]
        g_inv = jnp.exp(-cum)
        to_end = jnp.exp(cum_end - cum)
        beta = kk * a
        al_h = -kk * jnp.exp(cum_prev)
        r_h = r * jnp.exp(cum)
        be_c = beta * g_inv
        k_c = k * g_inv
        be_t = (beta * to_end).T
        k_t = (k * to_end).T
        gc_ref[c] = jnp.exp(cum.T[:, c_len - 1:c_len])
        probs = []
        for h in range(hp):
            hs = slice(h * n, (h + 1) * n)
            probs.append(dict(c=c, h=h, ah=al_h[:, hs], rh=r_h[:, hs], vh=v[:, hs],
                              bc=be_c[:, hs], kc=k_c[:, hs], bt=be_t[hs, :], kt=k_t[hs, :]))
        return probs

    def phase1(i, carry):
        probs = []
        for u in range(RW_UNROLL):
            probs += prep(i * RW_UNROLL + u)
        for q in probs:
            g = _dot_nt(jnp.concatenate([q["ah"], q["rh"]], axis=0),
                        jnp.concatenate([q["bc"], q["kc"]], axis=0))
            g = jnp.where(gmask, g, 0.0)
            q["a_ak"] = g[:c_len, c_len:]
            q["lower"] = g[c_len:, :]
            q["p"] = g[:c_len, :c_len]
            q["t"] = eye + q["p"]
        for _ in range(5):
            for q in probs:
                q["p"] = _dot(q["p"], q["p"])
            for q in probs:
                q["t"] = q["t"] + _dot(q["t"], q["p"])
        for q in probs:
            q["av"] = _dot(q["a_ak"], q["vh"])
        for q in probs:
            q["aw"] = _dot(q["t"], jnp.concatenate([q["ah"], q["av"]], axis=1))
        for q in probs:
            z = jnp.concatenate(
                [q["aw"], jnp.concatenate([jnp.zeros((c_len, n), F32), q["vh"]], axis=1)], axis=0)
            lhs = jnp.concatenate(
                [q["lower"], jnp.concatenate([q["bt"], q["kt"]], axis=1)], axis=0)
            f = _dot(lhs, z)
            rm = f[:, :n] + jnp.concatenate([q["rh"], jnp.zeros((n, n), F32)], axis=0)
            rm_ref[q["c"], q["h"]] = rm.astype(BF16)
            yn_ref[q["c"], q["h"]] = f[:, n:]
        return carry

    lax.fori_loop(0, nc // RW_UNROLL, phase1, 0)

    def phase2(c, carry):
        t0 = pl.multiple_of(c * c_len, c_len)
        gcol = gc_ref[c]
        ss = [st_ref[h] for h in range(hp)]
        fs = [jnp.dot(rm_ref[c, h], ss[h].astype(BF16), preferred_element_type=F32)
              for h in range(hp)]
        ys = []
        for h in range(hp):
            yn = yn_ref[c, h]
            ys.append(fs[h][:c_len] + yn[:c_len])
            st_ref[h] = gcol[h * n:(h + 1) * n] * ss[h] + fs[h][c_len:] + yn[c_len:]
        s_y[pl.ds(t0, c_len), :] = jnp.concatenate(ys, axis=1)
        return carry

    lax.fori_loop(0, nc, phase2, 0)

    def post(i, carry):
        t0 = pl.multiple_of(i * RW_PREP, RW_PREP)
        sl = pl.ds(t0, RW_PREP)
        y = s_y[sl, :]
        mean = _dot_hilo(y, seg_ones) * (1.0 / n)
        yc = y - mean
        var = _dot_hilo(yc * yc, seg_ones) * (1.0 / n)
        yn = yc * lax.rsqrt(var + RWKV_GN_EPS) * row(RV_LNW) + row(RV_LNB)
        ya_ref[0, sl, :] = ((yn + s_bonus[sl, :]) * s_gate[sl, :]).astype(BF16)
        return carry

    lax.fori_loop(0, tb // RW_PREP, post, 0)


def rwkv7_mixer(p, mu, w_up, a_up, g_up, vecs, r_k, v_first, vres_params):
    bsz, seq, _ = p.shape
    vres = vres_params is not None
    tb = SEQ_TB
    nc = tb // RW_C
    hp = RW_LANES // N_A
    pad_rows = lambda w, before, total: jnp.pad(w, ((before, total - before - w.shape[0]), (0, 0))).astype(BF16)
    rows = [mu[:D_A], mu[D_A:2 * D_A], mu[2 * D_A:3 * D_A]] + [vecs[i] for i in range(6)] + [r_k.reshape(D_A)]
    rows.append(vres_params[2] if vres else jnp.zeros((D_A,), F32))
    table = jnp.pad(jnp.stack(rows), ((0, RV_ROWS - len(rows)), (0, 0)))
    mu_wa = mu[3 * D_A:3 * D_A + R_W + R_A].reshape(1, LANES)
    mu_g = jnp.pad(mu[3 * D_A + R_W + R_A:], (0, 256 - R_G)).reshape(1, 256)
    blk = lambda w, f: pl.BlockSpec((1, tb, w), f)
    col = lambda w: pl.BlockSpec((w.shape[0], RW_LANES), lambda b, h, t: (0, h))
    full = lambda w: pl.BlockSpec(w.shape, lambda b, h, t: (0, 0))
    q = RW_LANES
    wup = pad_rows(w_up, 0, LANES)
    aup = pad_rows(a_up, R_W, LANES)
    gup = pad_rows(g_up, 0, 256)
    in_specs = [blk(q, lambda b, h, t: (b, t, _block_index(OFF_RKV, q) + h)),
                blk(q, lambda b, h, t: (b, t, _block_index(OFF_RKV + D_A, q) + h)),
                blk(q, lambda b, h, t: (b, t, _block_index(OFF_RKV + 2 * D_A, q) + h)),
                blk(256, lambda b, h, t: (b, t, _block_index(OFF_G, 256))),
                blk(LANES, lambda b, h, t: (b, t, _block_index(OFF_WA, LANES)))]
    args = [p, p, p, p, p]
    if vres:
        mu_dv = jnp.pad(vres_params[0], (H_B, LANES - H_B - R_V)).reshape(1, LANES)
        vup = pad_rows(vres_params[1], H_B, LANES)
        in_specs += [blk(LANES, lambda b, h, t: (b, t, _block_index(OFF_DT, LANES))),
                     blk(q, lambda b, h, t: (b, t, h))]
        args += [p, v_first]
    in_specs += [col(table), full(mu_wa), full(mu_g)]
    args += [table, mu_wa, mu_g]
    if vres:
        in_specs.append(full(mu_dv))
        args.append(mu_dv)
    in_specs += [col(wup), col(aup), col(gup)]
    args += [wup, aup, gup]
    if vres:
        in_specs.append(col(vup))
        args.append(vup)
    out_blk = blk(q, lambda b, h, t: (b, t, h))
    out_shape = [jax.ShapeDtypeStruct((bsz, seq, D_A), BF16)]
    out_specs = [out_blk]
    if not vres:
        out_shape.append(jax.ShapeDtypeStruct((bsz, seq, D_A), F32))
        out_specs.append(out_blk)
    carries = [pltpu.VMEM((SUBLANES, w), F32) for w in (q, q, q, 256, LANES) + ((LANES,) if vres else ())]
    outs = pl.pallas_call(
        functools.partial(_rwkv_kernel, nc=nc, vres=vres),
        out_shape=tuple(out_shape),
        grid=(bsz, D_A // q, seq // tb),
        in_specs=in_specs,
        out_specs=tuple(out_specs),
        scratch_shapes=[
            pltpu.VMEM((hp, N_A, N_A), F32),
            pltpu.VMEM((nc, hp, RW_C + N_A, N_A), BF16),
            pltpu.VMEM((nc, hp, RW_C + N_A, N_A), F32),
            pltpu.VMEM((nc, q, 1), F32),
        ] + carries + [pltpu.VMEM((tb, q), F32)] * 9,
        compiler_params=_cparams(("parallel", "parallel", "arbitrary")),
        name="rwkv7_mixer",
    )(*args)
    return (outs[0], v_first) if vres else (outs[0], outs[1])


SSD_TB = 256


def _ssd_kernel(z_ref, xs_ref, bm_ref, cm_ref, dt_ref, cw_ref, cb_ref, hd_ref, dsk_ref, ng_ref, o_ref,
                st_ref, c_xs, c_bm, c_cm, s_xs, s_bm, s_cm, s_dt, s_la):
    tb = SSD_TB
    raws = [(xs_ref, c_xs, 0, D_B), (bm_ref, c_bm, D_B, G_B * N_B), (cm_ref, c_cm, D_B + G_B * N_B, G_B * N_B)]

    @pl.when(pl.program_id(1) == 0)
    def _():
        st_ref[...] = jnp.zeros_like(st_ref)
        for _, c_ref, _, _ in raws:
            c_ref[...] = jnp.zeros_like(c_ref)

    for (p_ref, c_ref, off, width), s_ref in zip(raws, (s_xs, s_bm, s_cm)):
        x = p_ref[0]
        prev8 = c_ref[...]
        acc = cb_ref[:, off:off + width] + cw_ref[CONV_K - 1:CONV_K, off:off + width] * x
        for d in range(1, CONV_K):
            acc = acc + cw_ref[CONV_K - 1 - d:CONV_K - d, off:off + width] * _shift_rows(x, prev8, d)
        s_ref[...] = acc * _sigmoid(acc)
        c_ref[...] = p_ref[0, tb - SUBLANES:tb, :]
    dt = _softplus(dt_ref[0] + hd_ref[0:1, :])
    s_dt[...] = dt
    s_la[...] = dt * (-jnp.exp(hd_ref[1:2, :]))

    rowi = lax.broadcasted_iota(jnp.int32, (CHUNK, CHUNK), 0)
    coli = lax.broadcasted_iota(jnp.int32, (CHUNK, CHUNK), 1)
    causal = rowi >= coli
    tri_incl = causal.astype(BF16)

    def chunk(c, carry):
        t0 = pl.multiple_of(c * CHUNK, CHUNK)
        sl = pl.ds(t0, CHUNK)
        la = s_la[sl, :]
        la_hi = la.astype(BF16)
        la_lo = (la - la_hi.astype(F32)).astype(BF16)
        cum2 = jnp.dot(tri_incl, jnp.concatenate([la_hi, la_lo], axis=1), preferred_element_type=F32)
        cum = cum2[:, :LANES] + cum2[:, LANES:]
        cum_t = cum.T
        ecum = jnp.exp(cum)
        dend = jnp.exp(cum[CHUNK - 1:CHUNK, :] - cum)
        dtc = s_dt[sl, :]
        outs = []
        for g in range(G_B):
            cmg = s_cm[sl, g * N_B:(g + 1) * N_B]
            bmg = s_bm[sl, g * N_B:(g + 1) * N_B]
            cb = _dot_nt(cmg, bmg)
            bmt = bmg.T
            hs = [g * J_B + j for j in range(J_B)]
            xdt = [s_xs[sl, h * P_B:(h + 1) * P_B] * dtc[:, h:h + 1] for h in hs]
            sts = [st_ref[h] for h in hs]
            sc = [cb * jnp.exp(jnp.where(causal, cum[:, h:h + 1] - cum_t[h:h + 1, :], -1e30)) for h in hs]
            y_in = [_dot(s, x) for s, x in zip(sc, xdt)]
            y_st = [_dot(cmg, st) for st in sts]
            upd = [_dot(bmt, x * dend[:, h:h + 1]) for x, h in zip(xdt, hs)]
            for j, h in enumerate(hs):
                st_ref[h] = sts[j] * ecum[CHUNK - 1:CHUNK, h:h + 1] + upd[j]
                outs.append(y_in[j] + y_st[j] * ecum[:, h:h + 1])
        y = jnp.concatenate(outs, axis=1)
        zc = z_ref[0, sl, :]
        y = (y + dsk_ref[...] * s_xs[sl, :]) * (zc * _sigmoid(zc))
        pieces = []
        gw = D_B // G_B
        for g in range(G_B):
            yg = y[:, g * gw:(g + 1) * gw]
            pieces.append(yg * lax.rsqrt(jnp.mean(yg * yg, axis=-1, keepdims=True) + MAMBA_NORM_EPS))
        o_ref[0, sl, :] = (jnp.concatenate(pieces, axis=1) * ng_ref[...]).astype(BF16)
        return carry

    lax.fori_loop(0, tb // CHUNK, chunk, 0)


def mamba2_mixer(p, conv_w, conv_b, head, norm_g):
    bsz, seq, _ = p.shape
    tb = SSD_TB
    hd = jnp.pad(head[:2], ((0, SUBLANES - 2), (0, LANES - H_B)))
    dsk = jnp.repeat(head[2], P_B).reshape(1, D_B)
    gw = G_B * N_B
    blk = lambda w, off: pl.BlockSpec((1, tb, w), lambda b, t: (b, t, _block_index(off, w)))
    full = lambda a: pl.BlockSpec(a.shape, lambda b, t: (0,) * a.ndim)
    cb = conv_b.reshape(1, C_CONV)
    ng = norm_g.reshape(1, D_B)
    return pl.pallas_call(
        _ssd_kernel,
        out_shape=jax.ShapeDtypeStruct((bsz, seq, D_B), BF16),
        grid=(bsz, seq // tb),
        in_specs=[blk(D_B, OFF_Z), blk(D_B, OFF_XS), blk(gw, OFF_BM), blk(gw, OFF_CM), blk(LANES, OFF_DT),
                  full(conv_w), full(cb), full(hd), full(dsk), full(ng)],
        out_specs=pl.BlockSpec((1, tb, D_B), lambda b, t: (b, t, 0)),
        scratch_shapes=[
            pltpu.VMEM((H_B, N_B, P_B), F32),
            pltpu.VMEM((SUBLANES, D_B), F32), pltpu.VMEM((SUBLANES, gw), F32), pltpu.VMEM((SUBLANES, gw), F32),
            pltpu.VMEM((tb, D_B), F32), pltpu.VMEM((tb, gw), F32), pltpu.VMEM((tb, gw), F32),
            pltpu.VMEM((tb, LANES), F32), pltpu.VMEM((tb, LANES), F32),
        ],
        compiler_params=_cparams(("parallel", "arbitrary")),
        name="mamba2_mixer",
    )(p, p, p, p, p, conv_w, cb, hd, dsk, ng)


def _lru_kernel(gate_ref, xr_ref, cw_ref, cb_ref, gw_ref, gb_ref, lam_ref, o_ref, c_x, h_ref):
    tb = SEQ_TB
    first = pl.program_id(2) == 0

    @pl.when(first)
    def _():
        c_x[...] = jnp.zeros_like(c_x)
        h_ref[...] = jnp.zeros_like(h_ref)

    x = xr_ref[0]
    prev8 = c_x[...]
    xb = cb_ref[...] + cw_ref[CONV_K - 1:CONV_K, :] * x
    for d in range(1, CONV_K):
        xb = xb + cw_ref[CONV_K - 1 - d:CONV_K - d, :] * _shift_rows(x, prev8, d)
    c_x[...] = xr_ref[0, tb - SUBLANES:tb, :]
    gates = _sigmoid(_dot(xb, gw_ref[0]) + gb_ref[0])
    r_gate, i_gate = gates[:, :BS_C], gates[:, BS_C:]
    log_a = -C_RG * r_gate * _softplus(-lam_ref[...])
    a = jnp.exp(log_a)
    mult = jnp.sqrt(-jnp.tanh(log_a) * (a * a + 1.0))
    rowi = lax.broadcasted_iota(jnp.int32, (tb, BS_C), 0)
    mult = jnp.where(jnp.logical_and(first, rowi == 0), 1.0, mult)
    b = mult * (i_gate * xb)
    d = 1
    while d < tb:
        if d < SUBLANES:
            a_s = jnp.where(rowi >= d, pltpu.roll(a, d, axis=0), 1.0)
            b_s = jnp.where(rowi >= d, pltpu.roll(b, d, axis=0), 0.0)
        else:
            a_s = jnp.concatenate([jnp.ones((d, BS_C), F32), a[:tb - d]], axis=0)
            b_s = jnp.concatenate([jnp.zeros((d, BS_C), F32), b[:tb - d]], axis=0)
        b = b + a * b_s
        a = a * a_s
        d *= 2
    h = b + a * h_ref[...]
    h_ref[...] = h[tb - 1:tb, :]
    g = gate_ref[0]
    gelu = 0.5 * g * (1.0 + jnp.tanh(0.7978845608028654 * (g + 0.044715 * (g * g * g))))
    o_ref[0] = (h * gelu).astype(BF16)


def rglru_mixer(p, conv_w, conv_b, gate_w, gate_b, lam):
    bsz, seq, _ = p.shape
    tb = SEQ_TB
    gw = jnp.concatenate([gate_w[0], gate_w[1]], axis=-1).astype(BF16)
    gb = jnp.concatenate([gate_b[0].reshape(H_C, 1, BS_C), gate_b[1].reshape(H_C, 1, BS_C)], axis=-1)
    vec = lambda a: pl.BlockSpec((a.shape[0], BS_C), lambda b, h, t: (0, h))
    cb = conv_b.reshape(1, D_RNN)
    lm = lam.reshape(1, D_RNN)
    return pl.pallas_call(
        _lru_kernel,
        out_shape=jax.ShapeDtypeStruct((bsz, seq, D_RNN), BF16),
        grid=(bsz, H_C, seq // tb),
        in_specs=[pl.BlockSpec((1, tb, BS_C), lambda b, h, t: (b, t, h)),
                  pl.BlockSpec((1, tb, BS_C), lambda b, h, t: (b, t, H_C + h)),
                  vec(conv_w), vec(cb),
                  pl.BlockSpec((1, BS_C, 2 * BS_C), lambda b, h, t: (h, 0, 0)),
                  pl.BlockSpec((1, 1, 2 * BS_C), lambda b, h, t: (h, 0, 0)),
                  vec(lm)],
        out_specs=pl.BlockSpec((1, tb, BS_C), lambda b, h, t: (b, t, h)),
        scratch_shapes=[pltpu.VMEM((SUBLANES, BS_C), F32), pltpu.VMEM((1, BS_C), F32)],
        compiler_params=_cparams(("parallel", "parallel", "arbitrary")),
        name="rglru_mixer",
    )(p, p, conv_w, cb, gw, gb, lm)


def _moe_ffn_kernel(be_ref, x_ref, wg_ref, wu_ref, wd_ref, o_ref, wgu_s, wd_s):
    i = pl.program_id(0)

    @pl.when(jnp.logical_or(i == 0, be_ref[i] != be_ref[jnp.maximum(i - 1, 0)]))
    def _():
        wgu_s[:, :D_EXPERT] = wg_ref[0].astype(BF16)
        wgu_s[:, D_EXPERT:] = wu_ref[0].astype(BF16)
        wd_s[...] = wd_ref[0].astype(BF16)

    gu = jnp.dot(x_ref[...], wgu_s[...], preferred_element_type=F32)
    g, u = gu[:, :D_EXPERT], gu[:, D_EXPERT:]
    hb = (g * jax.nn.sigmoid(g)) * u
    o_ref[...] = jnp.dot(hb.astype(BF16), wd_s[...], preferred_element_type=F32).astype(BF16)


def moe_ffn(block_exp, xg, w_gate, w_up, w_down):
    n_slots, d = xg.shape
    grid_spec = pltpu.PrefetchScalarGridSpec(
        num_scalar_prefetch=1,
        grid=(n_slots // MOE_BLOCK,),
        in_specs=[pl.BlockSpec((MOE_BLOCK, d), lambda i, be: (i, 0)),
                  pl.BlockSpec((1, d, D_EXPERT), lambda i, be: (be[i], 0, 0)),
                  pl.BlockSpec((1, d, D_EXPERT), lambda i, be: (be[i], 0, 0)),
                  pl.BlockSpec((1, D_EXPERT, d), lambda i, be: (be[i], 0, 0))],
        out_specs=pl.BlockSpec((MOE_BLOCK, d), lambda i, be: (i, 0)),
        scratch_shapes=[pltpu.VMEM((d, 2 * D_EXPERT), BF16), pltpu.VMEM((D_EXPERT, d), BF16)],
    )
    return pl.pallas_call(
        _moe_ffn_kernel,
        out_shape=jax.ShapeDtypeStruct((n_slots, d), BF16),
        grid_spec=grid_spec,
        compiler_params=_cparams(("arbitrary",)),
        name="moe_ffn",
    )(block_exp, xg, w_gate, w_up, w_down)


def hier_moe(h, norm_g, w_grp, b_grp, w_exp, b_exp, w_gate, w_up, w_down):
    t, d = h.shape
    w_r = jnp.pad(jnp.concatenate([w_grp, w_exp], axis=1), ((0, 0), (0, LANES - N_GROUPS_E - N_EXPERTS)))
    xn, logits = norm_router(h, norm_g, w_r.astype(BF16))
    g_logits = logits[:, :N_GROUPS_E] + b_grp.astype(F32)
    g_idx = jnp.argmax(g_logits, axis=-1)
    g_w = jnp.take_along_axis(jax.nn.softmax(g_logits, axis=-1), g_idx[:, None], axis=-1)
    e_logits = (logits[:, N_GROUPS_E:N_GROUPS_E + N_EXPERTS] + b_exp.astype(F32)).reshape(
        t, N_GROUPS_E, E_PER_GROUP)
    e_logits = jnp.take_along_axis(e_logits, g_idx[:, None, None], axis=1)[:, 0]
    e_top, e_loc = lax.top_k(e_logits, TOP_K_E)
    e_w = jax.nn.softmax(e_top, axis=-1) * g_w
    e_flat = (g_idx[:, None] * E_PER_GROUP + e_loc).reshape(-1).astype(jnp.int32)
    tok_flat = jnp.repeat(jnp.arange(t, dtype=jnp.int32), TOP_K_E)
    n_assign = t * TOP_K_E
    onehot = (e_flat[:, None] == jnp.arange(N_EXPERTS, dtype=jnp.int32)[None, :]).astype(jnp.int32)
    running = jnp.cumsum(onehot, axis=0)
    counts = running[-1]
    rank = jnp.take_along_axis(running, e_flat[:, None], axis=1)[:, 0] - 1
    padded = (counts + MOE_BLOCK - 1) // MOE_BLOCK * MOE_BLOCK
    pad_start = jnp.cumsum(padded) - padded
    dest = pad_start[e_flat] + rank
    n_slots = n_assign + N_EXPERTS * MOE_BLOCK
    n_blocks = n_slots // MOE_BLOCK
    slot_tok = jnp.full((n_slots,), t, jnp.int32).at[dest].set(tok_flat)
    block_start = jnp.arange(n_blocks, dtype=jnp.int32) * MOE_BLOCK
    block_exp = jnp.minimum(
        jnp.sum(block_start[:, None] >= (pad_start + padded)[None, :], axis=-1), N_EXPERTS - 1)
    x_pad = jnp.concatenate([xn, jnp.zeros((1, d), BF16)], axis=0)
    y = moe_ffn(block_exp.astype(jnp.int32), x_pad[slot_tok], w_gate, w_up, w_down)
    slot_of = dest.reshape(t, TOP_K_E)
    return h + (y[slot_of[:, 0]] * e_w[:, 0:1] + y[slot_of[:, 1]] * e_w[:, 1:2])


def _hybrid_w_in(w_in):
    d = w_in.shape[0]
    zeros = lambda n: jnp.zeros((d, n), w_in.dtype)
    c_wa, c_g = 3 * D_A, 3 * D_A + R_W + R_A
    dv = w_in[:, C_HY:C_HY + R_V] if w_in.shape[1] > C_HY else zeros(R_V)
    c_x = C_A + D_B
    return jnp.concatenate([
        w_in[:, C_A:c_x], w_in[:, c_x:c_x + D_B], w_in[:, :3 * D_A], w_in[:, c_x + D_B:c_x + C_CONV],
        w_in[:, c_g:C_A], zeros(256 - R_G), w_in[:, c_wa:c_g],
        w_in[:, c_x + C_CONV:C_HY], dv, zeros(LANES - H_B - R_V)], axis=1).astype(BF16)


def kernel(x, norm_mix, norm_ffn, norm_final, hy_w_in_first, hy_w_in_vres, rk_mu, rk_mu_v, rk_w_up, rk_a_up, rk_g_up, rk_vecs, rk_r_k, rk_v_up, rk_v0, mb_conv_w, mb_conv_b, mb_head, mb_norm, hy_w_out, lr_w_in, lr_conv_w, lr_conv_b, lr_gate_w, lr_gate_b, lr_lambda, lr_w_out, moe_w_grp, moe_b_grp, moe_w_exp, moe_b_exp, moe_w_gate, moe_w_up, moe_w_down):
    bsz, seq, d = x.shape
    t = bsz * seq
    h = x.reshape(t, d)
    v_first = None
    for layer in range(DEPTH):
        j = layer // 2
        if layer % 2 == 0:
            w_in = hy_w_in_first if j == 0 else hy_w_in_vres[j - 1]
            vres_params = None if j == 0 else (rk_mu_v[j - 1], rk_v_up[j - 1], rk_v0[j - 1])
            p = norm_matmul(h, norm_mix[layer], _hybrid_w_in(w_in)).reshape(bsz, seq, P_W)
            ya, v_first = rwkv7_mixer(p, rk_mu[j], rk_w_up[j], rk_a_up[j], rk_g_up[j], rk_vecs[j], rk_r_k[j],
                                      v_first, vres_params)
            yb = mamba2_mixer(p, mb_conv_w[j], mb_conv_b[j], mb_head[j], mb_norm[j])
            w_out = hy_w_out[j].astype(BF16)
            h = matmul_residual([ya.reshape(t, D_A), yb.reshape(t, D_B)], [w_out[:D_A], w_out[D_A:]], h)
        else:
            p = norm_matmul(h, norm_mix[layer], lr_w_in[j].astype(BF16)).reshape(bsz, seq, 2 * D_RNN)
            y = rglru_mixer(p, lr_conv_w[j], lr_conv_b[j], lr_gate_w[j], lr_gate_b[j], lr_lambda[j])
            h = matmul_residual([y.reshape(t, D_RNN)], [lr_w_out[j].astype(BF16)], h)
        h = hier_moe(h, norm_ffn[layer], moe_w_grp[layer], moe_b_grp[layer], moe_w_exp[layer],
                     moe_b_exp[layer], moe_w_gate[layer], moe_w_up[layer], moe_w_down[layer])
    return final_norm(h, norm_final).reshape(bsz, seq, d)
```

```python
import functools

import jax
import jax.numpy as jnp
from jax import lax
from jax.experimental import pallas as pl
from jax.experimental.pallas import tpu as pltpu

D_MODEL = 2048
DEPTH = 4
D_A = D_MODEL // 2
N_A = 64
H_A = D_A // N_A
R_W = 64
R_A = 64
R_V = 32
R_G = 160
RWKV_GN_EPS = 64e-5
C_A = 3 * D_A + R_W + R_A + R_G
D_B = D_MODEL
P_B = 64
H_B = D_B // P_B
N_B = 128
G_B = 4
J_B = H_B // G_B
CONV_K = 4
C_CONV = D_B + 2 * G_B * N_B
C_B = D_B + C_CONV + H_B
CHUNK = 128
MAMBA_NORM_EPS = 1e-5
C_HY = C_A + C_B
D_CAT = D_A + D_B
BS_C = 256
D_RNN = ((4 * D_MODEL // 3 + BS_C // 2) // BS_C) * BS_C
H_C = D_RNN // BS_C
C_RG = 8.0
N_GROUPS_E = 8
E_PER_GROUP = 8
N_EXPERTS = N_GROUPS_E * E_PER_GROUP
TOP_K_E = 2
D_EXPERT = D_MODEL // 8
MOE_BLOCK = 256
NORM_EPS = 1e-6
F32 = jnp.float32
BF16 = jnp.bfloat16

VMEM_LIMIT_BYTES = 56 * 1024 * 1024
LANES = 128
SUBLANES = 8
MM_TM = 1024
MM_TN = 512
SEQ_TB = 512

OFF_Z = 0
OFF_XS = OFF_Z + D_B
OFF_RKV = OFF_XS + D_B
OFF_BM = OFF_RKV + 3 * D_A
OFF_CM = OFF_BM + G_B * N_B
OFF_G = OFF_CM + G_B * N_B
OFF_WA = OFF_G + 256
OFF_DT = OFF_WA + LANES
P_W = OFF_DT + LANES


def _block_index(offset, width):
    assert offset % width == 0, (offset, width)
    return offset // width


def _cparams(sem):
    return pltpu.CompilerParams(dimension_semantics=sem, vmem_limit_bytes=VMEM_LIMIT_BYTES)


def _dot(a, b):
    return jnp.dot(a.astype(BF16), b.astype(BF16), preferred_element_type=F32)


def _dot_nt(a, b):
    return lax.dot_general(a.astype(BF16), b.astype(BF16), (((1,), (1,)), ((), ())),
                           preferred_element_type=F32)


def _dot_hilo(x, w):
    hi = x.astype(BF16)
    lo = (x - hi.astype(F32)).astype(BF16)
    return (jnp.dot(hi, w, preferred_element_type=F32) + jnp.dot(lo, w, preferred_element_type=F32))


def _shift_rows(x, prev8, d):
    ext = jnp.concatenate([prev8, x], axis=0)
    return pltpu.roll(ext, d, axis=0)[SUBLANES:]


def _sigmoid(x):
    return 1.0 / (1.0 + jnp.exp(-x))


def _softplus(x):
    return jnp.maximum(x, 0.0) + jnp.log1p(jnp.exp(-jnp.abs(x)))


def _segment_ones(width, seg):
    shift = seg.bit_length() - 1
    r = lax.shift_right_logical(lax.broadcasted_iota(jnp.int32, (width, width), 0), shift)
    c = lax.shift_right_logical(lax.broadcasted_iota(jnp.int32, (width, width), 1), shift)
    return (r == c).astype(BF16)


def _norm_mm_kernel(x_ref, g_ref, w_ref, o_ref, xn_ref):
    @pl.when(pl.program_id(1) == 0)
    def _():
        x = x_ref[...]
        y = x * lax.rsqrt(jnp.mean(x * x, axis=-1, keepdims=True) + NORM_EPS) * g_ref[...]
        xn_ref[...] = y.astype(BF16)

    o_ref[...] = jnp.dot(xn_ref[...], w_ref[...], preferred_element_type=F32)


def norm_matmul(x, g, w):
    m, k = x.shape
    n = w.shape[1]
    return pl.pallas_call(
        _norm_mm_kernel,
        out_shape=jax.ShapeDtypeStruct((m, n), F32),
        grid=(m // MM_TM, n // MM_TN),
        in_specs=[pl.BlockSpec((MM_TM, k), lambda i, j: (i, 0)),
                  pl.BlockSpec((1, k), lambda i, j: (0, 0)),
                  pl.BlockSpec((k, MM_TN), lambda i, j: (0, j))],
        out_specs=pl.BlockSpec((MM_TM, MM_TN), lambda i, j: (i, j)),
        scratch_shapes=[pltpu.VMEM((MM_TM, k), BF16)],
        compiler_params=_cparams(("parallel", "arbitrary")),
        name="norm_matmul",
    )(x, g.reshape(1, k), w)


def _mm_res_kernel(*refs):
    *xw, res_ref, o_ref = refs
    acc = res_ref[...]
    for x_ref, w_ref in zip(xw[0::2], xw[1::2]):
        acc = acc + jnp.dot(x_ref[...], w_ref[...], preferred_element_type=F32)
    o_ref[...] = acc


def matmul_residual(xs, ws, res):
    m, n = res.shape
    in_specs, args = [], []
    for x, w in zip(xs, ws):
        in_specs += [pl.BlockSpec((MM_TM, x.shape[1]), lambda i, j: (i, 0)),
                     pl.BlockSpec((w.shape[0], MM_TN), lambda i, j: (0, j))]
        args += [x, w]
    in_specs.append(pl.BlockSpec((MM_TM, MM_TN), lambda i, j: (i, j)))
    return pl.pallas_call(
        _mm_res_kernel,
        out_shape=jax.ShapeDtypeStruct((m, n), F32),
        grid=(m // MM_TM, n // MM_TN),
        in_specs=in_specs,
        out_specs=pl.BlockSpec((MM_TM, MM_TN), lambda i, j: (i, j)),
        compiler_params=_cparams(("parallel", "arbitrary")),
        name="matmul_residual",
    )(*args, res)


def _norm_router_kernel(x_ref, g_ref, w_ref, xn_ref, lg_ref):
    x = x_ref[...]
    y = (x * lax.rsqrt(jnp.mean(x * x, axis=-1, keepdims=True) + NORM_EPS) * g_ref[...]).astype(BF16)
    xn_ref[...] = y
    lg_ref[...] = jnp.dot(y, w_ref[...], preferred_element_type=F32)


def norm_router(x, g, w):
    m, k = x.shape
    return pl.pallas_call(
        _norm_router_kernel,
        out_shape=(jax.ShapeDtypeStruct((m, k), BF16), jax.ShapeDtypeStruct((m, LANES), F32)),
        grid=(m // MM_TM,),
        in_specs=[pl.BlockSpec((MM_TM, k), lambda i: (i, 0)),
                  pl.BlockSpec((1, k), lambda i: (0, 0)),
                  pl.BlockSpec((k, LANES), lambda i: (0, 0))],
        out_specs=(pl.BlockSpec((MM_TM, k), lambda i: (i, 0)),
                   pl.BlockSpec((MM_TM, LANES), lambda i: (i, 0))),
        compiler_params=_cparams(("parallel",)),
        name="norm_router",
    )(x, g.reshape(1, k), w)


def _final_norm_kernel(x_ref, g_ref, o_ref):
    x = x_ref[...]
    o_ref[...] = x * lax.rsqrt(jnp.mean(x * x, axis=-1, keepdims=True) + NORM_EPS) * g_ref[...]


def final_norm(x, g):
    m, k = x.shape
    return pl.pallas_call(
        _final_norm_kernel,
        out_shape=jax.ShapeDtypeStruct((m, k), F32),
        grid=(m // MM_TM,),
        in_specs=[pl.BlockSpec((MM_TM, k), lambda i: (i, 0)), pl.BlockSpec((1, k), lambda i: (0, 0))],
        out_specs=pl.BlockSpec((MM_TM, k), lambda i: (i, 0)),
        compiler_params=_cparams(("parallel",)),
        name="final_norm",
    )(x, g.reshape(1, k))


RW_C = 64
RW_LANES = 256
RW_UNROLL = 8
RW_PREP = 256
(RV_MU_R, RV_MU_K, RV_MU_V, RV_W0, RV_A0, RV_KK, RV_KA, RV_LNW, RV_LNB, RV_RK, RV_V0) = range(11)
RV_ROWS = 16
EXP_M05 = 0.6065306597126334


def _rwkv_kernel(*refs, nc, vres):
    if vres:
        (pr_ref, pk_ref, pv_ref, pg_ref, pwa_ref, pdv_ref, vf_ref, vec_ref, muwa_ref, mug_ref, mudv_ref,
         wup_ref, aup_ref, gup_ref, vup_ref, ya_ref,
         st_ref, rm_ref, yn_ref, gc_ref, c_r, c_k, c_v, c_g, c_wa, c_dv,
         s_r, s_lw, s_k, s_v, s_kk, s_a, s_gate, s_bonus, s_y) = refs
        vfo_ref = None
    else:
        (pr_ref, pk_ref, pv_ref, pg_ref, pwa_ref, vec_ref, muwa_ref, mug_ref,
         wup_ref, aup_ref, gup_ref, ya_ref, vfo_ref,
         st_ref, rm_ref, yn_ref, gc_ref, c_r, c_k, c_v, c_g, c_wa,
         s_r, s_lw, s_k, s_v, s_kk, s_a, s_gate, s_bonus, s_y) = refs
    c_len, n = RW_C, N_A
    hp = RW_LANES // n
    tb = nc * c_len
    raws = [(pr_ref, c_r), (pk_ref, c_k), (pv_ref, c_v), (pg_ref, c_g), (pwa_ref, c_wa)]
    if vres:
        raws.append((pdv_ref, c_dv))

    @pl.when(pl.program_id(2) == 0)
    def _():
        st_ref[...] = jnp.zeros_like(st_ref)
        for _, c_ref in raws:
            c_ref[...] = jnp.zeros_like(c_ref)

    seg_ones = _segment_ones(RW_LANES, n)
    vec = vec_ref[...]
    row = lambda i: vec[i:i + 1, :]

    def prep0(i, carry):
        t0 = pl.multiple_of(i * RW_PREP, RW_PREP)
        sl = pl.ds(t0, RW_PREP)
        tp = pl.multiple_of(jnp.maximum(t0 - SUBLANES, 0), SUBLANES)

        def mixed(p_ref, c_ref, mu):
            x = p_ref[0, sl, :]
            prev8 = jnp.where(i == 0, c_ref[...], p_ref[0, pl.ds(tp, SUBLANES), :])
            return x + mu * (_shift_rows(x, prev8, 1) - x)

        r = mixed(pr_ref, c_r, row(RV_MU_R))
        k = mixed(pk_ref, c_k, row(RV_MU_K))
        v = mixed(pv_ref, c_v, row(RV_MU_V))
        dg = mixed(pg_ref, c_g, mug_ref[...])
        wa = mixed(pwa_ref, c_wa, muwa_ref[...])
        s_lw[sl, :] = -EXP_M05 * _sigmoid(row(RV_W0) + _dot(jnp.tanh(wa), wup_ref[...]))
        a = _sigmoid(row(RV_A0) + _dot(wa, aup_ref[...]))
        s_a[sl, :] = a
        s_gate[sl, :] = _dot(_sigmoid(dg), gup_ref[...])
        if vres:
            dv = mixed(pdv_ref, c_dv, mudv_ref[...])
            v = v + (vf_ref[0, sl, :] - v) * _sigmoid(row(RV_V0) + _dot(dv, vup_ref[...]))
        else:
            vfo_ref[0, sl, :] = v
        kk = k * row(RV_KK)
        norm = jnp.sqrt(_dot_hilo(kk * kk, seg_ones))
        s_kk[sl, :] = kk / jnp.maximum(norm, 1e-12)
        k = k * (1.0 + (a - 1.0) * row(RV_KA))
        s_r[sl, :] = r
        s_k[sl, :] = k
        s_v[sl, :] = v
        s_bonus[sl, :] = _dot_hilo(r * k * row(RV_RK), seg_ones) * v
        return carry

    lax.fori_loop(0, tb // RW_PREP, prep0, 0)
    for p_ref, c_ref in raws:
        c_ref[...] = p_ref[0, tb - SUBLANES:tb, :]

    rowi = lax.broadcasted_iota(jnp.int32, (c_len, c_len), 0)
    coli = lax.broadcasted_iota(jnp.int32, (c_len, c_len), 1)
    tri_incl = (rowi >= coli).astype(BF16)
    eye = (rowi == coli).astype(F32)
    row2 = lax.broadcasted_iota(jnp.int32, (2 * c_len, 2 * c_len), 0)
    col2 = lax.broadcasted_iota(jnp.int32, (2 * c_len, 2 * c_len), 1)
    rr = jnp.where(row2 >= c_len, row2 - c_len, row2)
    cc = jnp.where(col2 >= c_len, col2 - c_len, col2)
    gmask = (rr - cc) >= jnp.where(row2 >= c_len, 0, 1)

    def prep(c):
        t0 = pl.multiple_of(c * c_len, c_len)
        sl = pl.ds(t0, c_len)
        lw = s_lw[sl, :]
        r = s_r[sl, :]
        k = s_k[sl, :]
        v = s_v[sl, :]
        kk = s_kk[sl, :]
        a = s_a[sl, :]
        lw_hi = lw.astype(BF16)
        lw_lo = (lw - lw_hi.astype(F32)).astype(BF16)
        cum2 = jnp.dot(tri_incl, jnp.concatenate([lw_hi, lw_lo], axis=1), preferred_element_type=F32)
        cum = cum2[:, :RW_LANES] + cum2[:, RW_LANES:]
        cum_prev = cum - lw
        cum_end = cum[c_len - 1:c_len, :]
        g_inv = jnp.exp(-cum)
        to_end = jnp.exp(cum_end - cum)
        beta = kk * a
        al_h = -kk * jnp.exp(cum_prev)
        r_h = r * jnp.exp(cum)
        be_c = beta * g_inv
        k_c = k * g_inv
        be_t = (beta * to_end).T
        k_t = (k * to_end).T
        gc_ref[c] = jnp.exp(cum.T[:, c_len - 1:c_len])
        probs = []
        for h in range(hp):
            hs = slice(h * n, (h + 1) * n)
            probs.append(dict(c=c, h=h, ah=al_h[:, hs], rh=r_h[:, hs], vh=v[:, hs],
                              bc=be_c[:, hs], kc=k_c[:, hs], bt=be_t[hs, :], kt=k_t[hs, :]))
        return probs

    def phase1(i, carry):
        probs = []
        for u in range(RW_UNROLL):
            probs += prep(i * RW_UNROLL + u)
        for q in probs:
            g = _dot_nt(jnp.concatenate([q["ah"], q["rh"]], axis=0),
                        jnp.concatenate([q["bc"], q["kc"]], axis=0))
            g = jnp.where(gmask, g, 0.0)
            q["a_ak"] = g[:c_len, c_len:]
            q["lower"] = g[c_len:, :]
            q["p"] = g[:c_len, :c_len]
            q["t"] = eye + q["p"]
        for _ in range(5):
            for q in probs:
                q["p"] = _dot(q["p"], q["p"])
            for q in probs:
                q["t"] = q["t"] + _dot(q["t"], q["p"])
        for q in probs:
            q["av"] = _dot(q["a_ak"], q["vh"])
        for q in probs:
            q["aw"] = _dot(q["t"], jnp.concatenate([q["ah"], q["av"]], axis=1))
        for q in probs:
            z = jnp.concatenate(
                [q["aw"], jnp.concatenate([jnp.zeros((c_len, n), F32), q["vh"]], axis=1)], axis=0)
            lhs = jnp.concatenate(
                [q["lower"], jnp.concatenate([q["bt"], q["kt"]], axis=1)], axis=0)
            f = _dot(lhs, z)
            rm = f[:, :n] + jnp.concatenate([q["rh"], jnp.zeros((n, n), F32)], axis=0)
            rm_ref[q["c"], q["h"]] = rm.astype(BF16)
            yn_ref[q["c"], q["h"]] = f[:, n:]
        return carry

    lax.fori_loop(0, nc // RW_UNROLL, phase1, 0)

    def phase2(c, carry):
        t0 = pl.multiple_of(c * c_len, c_len)
        gcol = gc_ref[c]
        ss = [st_ref[h] for h in range(hp)]
        fs = [jnp.dot(rm_ref[c, h], ss[h].astype(BF16), preferred_element_type=F32)
              for h in range(hp)]
        ys = []
        for h in range(hp):
            yn = yn_ref[c, h]
            ys.append(fs[h][:c_len] + yn[:c_len])
            st_ref[h] = gcol[h * n:(h + 1) * n] * ss[h] + fs[h][c_len:] + yn[c_len:]
        s_y[pl.ds(t0, c_len), :] = jnp.concatenate(ys, axis=1)
        return carry

    lax.fori_loop(0, nc, phase2, 0)

    def post(i, carry):
        t0 = pl.multiple_of(i * RW_PREP, RW_PREP)
        sl = pl.ds(t0, RW_PREP)
        y = s_y[sl, :]
        mean = _dot_hilo(y, seg_ones) * (1.0 / n)
        yc = y - mean
        var = _dot_hilo(yc * yc, seg_ones) * (1.0 / n)
        yn = yc * lax.rsqrt(var + RWKV_GN_EPS) * row(RV_LNW) + row(RV_LNB)
        ya_ref[0, sl, :] = ((yn + s_bonus[sl, :]) * s_gate[sl, :]).astype(BF16)
        return carry

    lax.fori_loop(0, tb // RW_PREP, post, 0)


def rwkv7_mixer(p, mu, w_up, a_up, g_up, vecs, r_k, v_first, vres_params):
    bsz, seq, _ = p.shape
    vres = vres_params is not None
    tb = SEQ_TB
    nc = tb // RW_C
    hp = RW_LANES // N_A
    pad_rows = lambda w, before, total: jnp.pad(w, ((before, total - before - w.shape[0]), (0, 0))).astype(BF16)
    rows = [mu[:D_A], mu[D_A:2 * D_A], mu[2 * D_A:3 * D_A]] + [vecs[i] for i in range(6)] + [r_k.reshape(D_A)]
    rows.append(vres_params[2] if vres else jnp.zeros((D_A,), F32))
    table = jnp.pad(jnp.stack(rows), ((0, RV_ROWS - len(rows)), (0, 0)))
    mu_wa = mu[3 * D_A:3 * D_A + R_W + R_A].reshape(1, LANES)
    mu_g = jnp.pad(mu[3 * D_A + R_W + R_A:], (0, 256 - R_G)).reshape(1, 256)
    blk = lambda w, f: pl.BlockSpec((1, tb, w), f)
    col = lambda w: pl.BlockSpec((w.shape[0], RW_LANES), lambda b, h, t: (0, h))
    full = lambda w: pl.BlockSpec(w.shape, lambda b, h, t: (0, 0))
    q = RW_LANES
    wup = pad_rows(w_up, 0, LANES)
    aup = pad_rows(a_up, R_W, LANES)
    gup = pad_rows(g_up, 0, 256)
    in_specs = [blk(q, lambda b, h, t: (b, t, _block_index(OFF_RKV, q) + h)),
                blk(q, lambda b, h, t: (b, t, _block_index(OFF_RKV + D_A, q) + h)),
                blk(q, lambda b, h, t: (b, t, _block_index(OFF_RKV + 2 * D_A, q) + h)),
                blk(256, lambda b, h, t: (b, t, _block_index(OFF_G, 256))),
                blk(LANES, lambda b, h, t: (b, t, _block_index(OFF_WA, LANES)))]
    args = [p, p, p, p, p]
    if vres:
        mu_dv = jnp.pad(vres_params[0], (H_B, LANES - H_B - R_V)).reshape(1, LANES)
        vup = pad_rows(vres_params[1], H_B, LANES)
        in_specs += [blk(LANES, lambda b, h, t: (b, t, _block_index(OFF_DT, LANES))),
                     blk(q, lambda b, h, t: (b, t, h))]
        args += [p, v_first]
    in_specs += [col(table), full(mu_wa), full(mu_g)]
    args += [table, mu_wa, mu_g]
    if vres:
        in_specs.append(full(mu_dv))
        args.append(mu_dv)
    in_specs += [col(wup), col(aup), col(gup)]
    args += [wup, aup, gup]
    if vres:
        in_specs.append(col(vup))
        args.append(vup)
    out_blk = blk(q, lambda b, h, t: (b, t, h))
    out_shape = [jax.ShapeDtypeStruct((bsz, seq, D_A), BF16)]
    out_specs = [out_blk]
    if not vres:
        out_shape.append(jax.ShapeDtypeStruct((bsz, seq, D_A), F32))
        out_specs.append(out_blk)
    carries = [pltpu.VMEM((SUBLANES, w), F32) for w in (q, q, q, 256, LANES) + ((LANES,) if vres else ())]
    outs = pl.pallas_call(
        functools.partial(_rwkv_kernel, nc=nc, vres=vres),
        out_shape=tuple(out_shape),
        grid=(bsz, D_A // q, seq // tb),
        in_specs=in_specs,
        out_specs=tuple(out_specs),
        scratch_shapes=[
            pltpu.VMEM((hp, N_A, N_A), F32),
            pltpu.VMEM((nc, hp, RW_C + N_A, N_A), BF16),
            pltpu.VMEM((nc, hp, RW_C + N_A, N_A), F32),
            pltpu.VMEM((nc, q, 1), F32),
        ] + carries + [pltpu.VMEM((tb, q), F32)] * 9,
        compiler_params=_cparams(("parallel", "parallel", "arbitrary")),
        name="rwkv7_mixer",
    )(*args)
    return (outs[0], v_first) if vres else (outs[0], outs[1])


SSD_TB = 256


def _ssd_kernel(z_ref, xs_ref, bm_ref, cm_ref, dt_ref, cw_ref, cb_ref, hd_ref, dsk_ref, ng_ref, o_ref,
                st_ref, c_xs, c_bm, c_cm, s_xs, s_bm, s_cm, s_dt, s_la):
    tb = SSD_TB
    raws = [(xs_ref, c_xs, 0, D_B), (bm_ref, c_bm, D_B, G_B * N_B), (cm_ref, c_cm, D_B + G_B * N_B, G_B * N_B)]

    @pl.when(pl.program_id(1) == 0)
    def _():
        st_ref[...] = jnp.zeros_like(st_ref)
        for _, c_ref, _, _ in raws:
            c_ref[...] = jnp.zeros_like(c_ref)

    for (p_ref, c_ref, off, width), s_ref in zip(raws, (s_xs, s_bm, s_cm)):
        x = p_ref[0]
        prev8 = c_ref[...]
        acc = cb_ref[:, off:off + width] + cw_ref[CONV_K - 1:CONV_K, off:off + width] * x
        for d in range(1, CONV_K):
            acc = acc + cw_ref[CONV_K - 1 - d:CONV_K - d, off:off + width] * _shift_rows(x, prev8, d)
        s_ref[...] = acc * _sigmoid(acc)
        c_ref[...] = p_ref[0, tb - SUBLANES:tb, :]
    dt = _softplus(dt_ref[0] + hd_ref[0:1, :])
    s_dt[...] = dt
    s_la[...] = dt * (-jnp.exp(hd_ref[1:2, :]))

    rowi = lax.broadcasted_iota(jnp.int32, (CHUNK, CHUNK), 0)
    coli = lax.broadcasted_iota(jnp.int32, (CHUNK, CHUNK), 1)
    causal = rowi >= coli
    tri_incl = causal.astype(BF16)
    first_head = coli < P_B
    expand = (lax.shift_right_logical(lax.broadcasted_iota(jnp.int32, (LANES, D_B), 1), P_B.bit_length() - 1)
              == lax.broadcasted_iota(jnp.int32, (LANES, D_B), 0)).astype(BF16)
    pw = 2 * P_B

    def chunk(c, carry):
        t0 = pl.multiple_of(c * CHUNK, CHUNK)
        sl = pl.ds(t0, CHUNK)
        la = s_la[sl, :]
        la_hi = la.astype(BF16)
        la_lo = (la - la_hi.astype(F32)).astype(BF16)
        cum2 = jnp.dot(tri_incl, jnp.concatenate([la_hi, la_lo], axis=1), preferred_element_type=F32)
        cum = cum2[:, :LANES] + cum2[:, LANES:]
        cum_t = cum.T
        ecum_f = _dot_hilo(jnp.exp(cum), expand)
        dend_f = _dot_hilo(jnp.exp(cum[CHUNK - 1:CHUNK, :] - cum), expand)
        xdt_f = s_xs[sl, :] * _dot_hilo(s_dt[sl, :], expand)
        xde_f = xdt_f * dend_f
        outs = []
        for g in range(G_B):
            cmg = s_cm[sl, g * N_B:(g + 1) * N_B]
            bmg = s_bm[sl, g * N_B:(g + 1) * N_B]
            cb = _dot_nt(cmg, bmg)
            bmt = bmg.T
            pairs = [g * (J_B // 2) + m for m in range(J_B // 2)]
            lanes = [slice(q * pw, (q + 1) * pw) for q in pairs]
            sc = [[cb * jnp.exp(jnp.where(causal, cum[:, h:h + 1] - cum_t[h:h + 1, :], -1e30))
                   for h in (2 * q, 2 * q + 1)] for q in pairs]
            sts = [st_ref[q] for q in pairs]
            y_in = [_dot(jnp.concatenate(s2, axis=1),
                         jnp.concatenate([jnp.where(first_head, xdt_f[:, ln], 0.0),
                                          jnp.where(first_head, 0.0, xdt_f[:, ln])], axis=0))
                    for s2, ln in zip(sc, lanes)]
            y_st = [_dot(cmg, st) for st in sts]
            upd = [_dot(bmt, xde_f[:, ln]) for ln in lanes]
            for j, (q, ln) in enumerate(zip(pairs, lanes)):
                st_ref[q] = sts[j] * ecum_f[CHUNK - 1:CHUNK, ln] + upd[j]
                outs.append(y_in[j] + y_st[j] * ecum_f[:, ln])
        y = jnp.concatenate(outs, axis=1)
        zc = z_ref[0, sl, :]
        y = (y + dsk_ref[...] * s_xs[sl, :]) * (zc * _sigmoid(zc))
        pieces = []
        gw = D_B // G_B
        for g in range(G_B):
            yg = y[:, g * gw:(g + 1) * gw]
            pieces.append(yg * lax.rsqrt(jnp.mean(yg * yg, axis=-1, keepdims=True) + MAMBA_NORM_EPS))
        o_ref[0, sl, :] = (jnp.concatenate(pieces, axis=1) * ng_ref[...]).astype(BF16)
        return carry

    lax.fori_loop(0, tb // CHUNK, chunk, 0)


def mamba2_mixer(p, conv_w, conv_b, head, norm_g):
    bsz, seq, _ = p.shape
    tb = SSD_TB
    hd = jnp.pad(head[:2], ((0, SUBLANES - 2), (0, LANES - H_B)))
    dsk = jnp.repeat(head[2], P_B).reshape(1, D_B)
    gw = G_B * N_B
    blk = lambda w, off: pl.BlockSpec((1, tb, w), lambda b, t: (b, t, _block_index(off, w)))
    full = lambda a: pl.BlockSpec(a.shape, lambda b, t: (0,) * a.ndim)
    cb = conv_b.reshape(1, C_CONV)
    ng = norm_g.reshape(1, D_B)
    return pl.pallas_call(
        _ssd_kernel,
        out_shape=jax.ShapeDtypeStruct((bsz, seq, D_B), BF16),
        grid=(bsz, seq // tb),
        in_specs=[blk(D_B, OFF_Z), blk(D_B, OFF_XS), blk(gw, OFF_BM), blk(gw, OFF_CM), blk(LANES, OFF_DT),
                  full(conv_w), full(cb), full(hd), full(dsk), full(ng)],
        out_specs=pl.BlockSpec((1, tb, D_B), lambda b, t: (b, t, 0)),
        scratch_shapes=[
            pltpu.VMEM((H_B // 2, N_B, 2 * P_B), F32),
            pltpu.VMEM((SUBLANES, D_B), F32), pltpu.VMEM((SUBLANES, gw), F32), pltpu.VMEM((SUBLANES, gw), F32),
            pltpu.VMEM((tb, D_B), F32), pltpu.VMEM((tb, gw), F32), pltpu.VMEM((tb, gw), F32),
            pltpu.VMEM((tb, LANES), F32), pltpu.VMEM((tb, LANES), F32),
        ],
        compiler_params=_cparams(("parallel", "arbitrary")),
        name="mamba2_mixer",
    )(p, p, p, p, p, conv_w, cb, hd, dsk, ng)


def _lru_kernel(gate_ref, xr_ref, cw_ref, cb_ref, gw_ref, gb_ref, lam_ref, o_ref, c_x, h_ref):
    tb = SEQ_TB
    first = pl.program_id(2) == 0

    @pl.when(first)
    def _():
        c_x[...] = jnp.zeros_like(c_x)
        h_ref[...] = jnp.zeros_like(h_ref)

    x = xr_ref[0]
    prev8 = c_x[...]
    xb = cb_ref[...] + cw_ref[CONV_K - 1:CONV_K, :] * x
    for d in range(1, CONV_K):
        xb = xb + cw_ref[CONV_K - 1 - d:CONV_K - d, :] * _shift_rows(x, prev8, d)
    c_x[...] = xr_ref[0, tb - SUBLANES:tb, :]
    gates = _sigmoid(_dot(xb, gw_ref[0]) + gb_ref[0])
    r_gate, i_gate = gates[:, :BS_C], gates[:, BS_C:]
    log_a = -C_RG * r_gate * _softplus(-lam_ref[...])
    a = jnp.exp(log_a)
    mult = jnp.sqrt(-jnp.tanh(log_a) * (a * a + 1.0))
    rowi = lax.broadcasted_iota(jnp.int32, (tb, BS_C), 0)
    mult = jnp.where(jnp.logical_and(first, rowi == 0), 1.0, mult)
    b = mult * (i_gate * xb)
    d = 1
    while d < tb:
        if d < SUBLANES:
            a_s = jnp.where(rowi >= d, pltpu.roll(a, d, axis=0), 1.0)
            b_s = jnp.where(rowi >= d, pltpu.roll(b, d, axis=0), 0.0)
        else:
            a_s = jnp.concatenate([jnp.ones((d, BS_C), F32), a[:tb - d]], axis=0)
            b_s = jnp.concatenate([jnp.zeros((d, BS_C), F32), b[:tb - d]], axis=0)
        b = b + a * b_s
        a = a * a_s
        d *= 2
    h = b + a * h_ref[...]
    h_ref[...] = h[tb - 1:tb, :]
    g = gate_ref[0]
    gelu = 0.5 * g * (1.0 + jnp.tanh(0.7978845608028654 * (g + 0.044715 * (g * g * g))))
    o_ref[0] = (h * gelu).astype(BF16)


def rglru_mixer(p, conv_w, conv_b, gate_w, gate_b, lam):
    bsz, seq, _ = p.shape
    tb = SEQ_TB
    gw = jnp.concatenate([gate_w[0], gate_w[1]], axis=-1).astype(BF16)
    gb = jnp.concatenate([gate_b[0].reshape(H_C, 1, BS_C), gate_b[1].reshape(H_C, 1, BS_C)], axis=-1)
    vec = lambda a: pl.BlockSpec((a.shape[0], BS_C), lambda b, h, t: (0, h))
    cb = conv_b.reshape(1, D_RNN)
    lm = lam.reshape(1, D_RNN)
    return pl.pallas_call(
        _lru_kernel,
        out_shape=jax.ShapeDtypeStruct((bsz, seq, D_RNN), BF16),
        grid=(bsz, H_C, seq // tb),
        in_specs=[pl.BlockSpec((1, tb, BS_C), lambda b, h, t: (b, t, h)),
                  pl.BlockSpec((1, tb, BS_C), lambda b, h, t: (b, t, H_C + h)),
                  vec(conv_w), vec(cb),
                  pl.BlockSpec((1, BS_C, 2 * BS_C), lambda b, h, t: (h, 0, 0)),
                  pl.BlockSpec((1, 1, 2 * BS_C), lambda b, h, t: (h, 0, 0)),
                  vec(lm)],
        out_specs=pl.BlockSpec((1, tb, BS_C), lambda b, h, t: (b, t, h)),
        scratch_shapes=[pltpu.VMEM((SUBLANES, BS_C), F32), pltpu.VMEM((1, BS_C), F32)],
        compiler_params=_cparams(("parallel", "parallel", "arbitrary")),
        name="rglru_mixer",
    )(p, p, conv_w, cb, gw, gb, lm)


def _moe_ffn_kernel(be_ref, x_ref, wg_ref, wu_ref, wd_ref, o_ref, wgu_s, wd_s):
    i = pl.program_id(0)

    @pl.when(jnp.logical_or(i == 0, be_ref[i] != be_ref[jnp.maximum(i - 1, 0)]))
    def _():
        wgu_s[:, :D_EXPERT] = wg_ref[0, 0].astype(BF16)
        wgu_s[:, D_EXPERT:] = wu_ref[0, 0].astype(BF16)
        wd_s[...] = wd_ref[0, 0].astype(BF16)

    gu = jnp.dot(x_ref[...], wgu_s[...], preferred_element_type=F32)
    g, u = gu[:, :D_EXPERT], gu[:, D_EXPERT:]
    hb = (g * jax.nn.sigmoid(g)) * u
    o_ref[...] = jnp.dot(hb.astype(BF16), wd_s[...], preferred_element_type=F32).astype(BF16)


def moe_ffn(block_exp, xg, w_gate, w_up, w_down, layer):
    n_slots, d = xg.shape
    grid_spec = pltpu.PrefetchScalarGridSpec(
        num_scalar_prefetch=1,
        grid=(n_slots // MOE_BLOCK,),
        in_specs=[pl.BlockSpec((MOE_BLOCK, d), lambda i, be: (i, 0)),
                  pl.BlockSpec((1, 1, d, D_EXPERT), lambda i, be: (layer, be[i], 0, 0)),
                  pl.BlockSpec((1, 1, d, D_EXPERT), lambda i, be: (layer, be[i], 0, 0)),
                  pl.BlockSpec((1, 1, D_EXPERT, d), lambda i, be: (layer, be[i], 0, 0))],
        out_specs=pl.BlockSpec((MOE_BLOCK, d), lambda i, be: (i, 0)),
        scratch_shapes=[pltpu.VMEM((d, 2 * D_EXPERT), BF16), pltpu.VMEM((D_EXPERT, d), BF16)],
    )
    return pl.pallas_call(
        _moe_ffn_kernel,
        out_shape=jax.ShapeDtypeStruct((n_slots, d), BF16),
        grid_spec=grid_spec,
        compiler_params=_cparams(("arbitrary",)),
        name="moe_ffn",
    )(block_exp, xg, w_gate, w_up, w_down)


def hier_moe(h, norm_g, w_grp, b_grp, w_exp, b_exp, w_gate, w_up, w_down, layer):
    t, d = h.shape
    w_r = jnp.pad(jnp.concatenate([w_grp, w_exp], axis=1), ((0, 0), (0, LANES - N_GROUPS_E - N_EXPERTS)))
    xn, logits = norm_router(h, norm_g, w_r.astype(BF16))
    g_logits = logits[:, :N_GROUPS_E] + b_grp.astype(F32)
    g_idx = jnp.argmax(g_logits, axis=-1)
    g_w = jnp.take_along_axis(jax.nn.softmax(g_logits, axis=-1), g_idx[:, None], axis=-1)
    e_logits = (logits[:, N_GROUPS_E:N_GROUPS_E + N_EXPERTS] + b_exp.astype(F32)).reshape(
        t, N_GROUPS_E, E_PER_GROUP)
    e_logits = jnp.take_along_axis(e_logits, g_idx[:, None, None], axis=1)[:, 0]
    e_top, e_loc = lax.top_k(e_logits, TOP_K_E)
    e_w = jax.nn.softmax(e_top, axis=-1) * g_w
    e_flat = (g_idx[:, None] * E_PER_GROUP + e_loc).reshape(-1).astype(jnp.int32)
    n_assign = t * TOP_K_E
    onehot = (e_flat[:, None] == jnp.arange(N_EXPERTS, dtype=jnp.int32)[None, :]).astype(jnp.int32)
    running = jnp.cumsum(onehot, axis=0)
    counts = running[-1]
    rank = jnp.take_along_axis(running, e_flat[:, None], axis=1)[:, 0] - 1
    padded = (counts + MOE_BLOCK - 1) // MOE_BLOCK * MOE_BLOCK
    raw_start = jnp.cumsum(counts) - counts
    pad_start = jnp.cumsum(padded) - padded
    dest = pad_start[e_flat] + rank
    n_slots = n_assign + N_EXPERTS * MOE_BLOCK
    n_blocks = n_slots // MOE_BLOCK
    block_start = jnp.arange(n_blocks, dtype=jnp.int32) * MOE_BLOCK
    block_exp = jnp.minimum(
        jnp.sum(block_start[:, None] >= (pad_start + padded)[None, :], axis=-1), N_EXPERTS - 1)
    order = jnp.argsort(e_flat)
    slot_e = jnp.repeat(block_exp, MOE_BLOCK)
    off = jnp.arange(n_slots, dtype=jnp.int32) - pad_start[slot_e]
    src = order[jnp.clip(raw_start[slot_e] + off, 0, n_assign - 1)]
    slot_tok = jnp.where(off < counts[slot_e], src // TOP_K_E, t).astype(jnp.int32)
    x_pad = jnp.concatenate([xn, jnp.zeros((1, d), BF16)], axis=0)
    y = moe_ffn(block_exp.astype(jnp.int32), x_pad[slot_tok], w_gate, w_up, w_down, layer)
    slot_of = dest.reshape(t, TOP_K_E)
    return h + (y[slot_of[:, 0]] * e_w[:, 0:1] + y[slot_of[:, 1]] * e_w[:, 1:2])


def _hybrid_w_in(w_in):
    d = w_in.shape[0]
    zeros = lambda n: jnp.zeros((d, n), w_in.dtype)
    c_wa, c_g = 3 * D_A, 3 * D_A + R_W + R_A
    dv = w_in[:, C_HY:C_HY + R_V] if w_in.shape[1] > C_HY else zeros(R_V)
    c_x = C_A + D_B
    return jnp.concatenate([
        w_in[:, C_A:c_x], w_in[:, c_x:c_x + D_B], w_in[:, :3 * D_A], w_in[:, c_x + D_B:c_x + C_CONV],
        w_in[:, c_g:C_A], zeros(256 - R_G), w_in[:, c_wa:c_g],
        w_in[:, c_x + C_CONV:C_HY], dv, zeros(LANES - H_B - R_V)], axis=1).astype(BF16)


def kernel(x, norm_mix, norm_ffn, norm_final, hy_w_in_first, hy_w_in_vres, rk_mu, rk_mu_v, rk_w_up, rk_a_up, rk_g_up, rk_vecs, rk_r_k, rk_v_up, rk_v0, mb_conv_w, mb_conv_b, mb_head, mb_norm, hy_w_out, lr_w_in, lr_conv_w, lr_conv_b, lr_gate_w, lr_gate_b, lr_lambda, lr_w_out, moe_w_grp, moe_b_grp, moe_w_exp, moe_b_exp, moe_w_gate, moe_w_up, moe_w_down):
    bsz, seq, d = x.shape
    t = bsz * seq
    h = x.reshape(t, d)
    v_first = None
    for layer in range(DEPTH):
        j = layer // 2
        if layer % 2 == 0:
            w_in = hy_w_in_first if j == 0 else hy_w_in_vres[j - 1]
            vres_params = None if j == 0 else (rk_mu_v[j - 1], rk_v_up[j - 1], rk_v0[j - 1])
            p = norm_matmul(h, norm_mix[layer], _hybrid_w_in(w_in)).reshape(bsz, seq, P_W)
            ya, v_first = rwkv7_mixer(p, rk_mu[j], rk_w_up[j], rk_a_up[j], rk_g_up[j], rk_vecs[j], rk_r_k[j],
                                      v_first, vres_params)
            yb = mamba2_mixer(p, mb_conv_w[j], mb_conv_b[j], mb_head[j], mb_norm[j])
            w_out = hy_w_out[j].astype(BF16)
            h = matmul_residual([ya.reshape(t, D_A), yb.reshape(t, D_B)], [w_out[:D_A], w_out[D_A:]], h)
        else:
            p = norm_matmul(h, norm_mix[layer], lr_w_in[j].astype(BF16)).reshape(bsz, seq, 2 * D_RNN)
            y = rglru_mixer(p, lr_conv_w[j], lr_conv_b[j], lr_gate_w[j], lr_gate_b[j], lr_lambda[j])
            h = matmul_residual([y.reshape(t, D_RNN)], [lr_w_out[j].astype(BF16)], h)
        h = hier_moe(h, norm_ffn[layer], moe_w_grp[layer], moe_b_grp[layer], moe_w_exp[layer],
                     moe_b_exp[layer], moe_w_gate, moe_w_up, moe_w_down, layer)
    return final_norm(h, norm_final).reshape(bsz, seq, d)
```

```python
import functools

import jax
import jax.numpy as jnp
from jax import lax
from jax.experimental import pallas as pl
from jax.experimental.pallas import tpu as pltpu

D_MODEL = 2048
DEPTH = 4
D_A = D_MODEL // 2
N_A = 64
H_A = D_A // N_A
R_W = 64
R_A = 64
R_V = 32
R_G = 160
RWKV_GN_EPS = 64e-5
C_A = 3 * D_A + R_W + R_A + R_G
D_B = D_MODEL
P_B = 64
H_B = D_B // P_B
N_B = 128
G_B = 4
J_B = H_B // G_B
CONV_K = 4
C_CONV = D_B + 2 * G_B * N_B
C_B = D_B + C_CONV + H_B
CHUNK = 128
MAMBA_NORM_EPS = 1e-5
C_HY = C_A + C_B
D_CAT = D_A + D_B
BS_C = 256
D_RNN = ((4 * D_MODEL // 3 + BS_C // 2) // BS_C) * BS_C
H_C = D_RNN // BS_C
C_RG = 8.0
N_GROUPS_E = 8
E_PER_GROUP = 8
N_EXPERTS = N_GROUPS_E * E_PER_GROUP
TOP_K_E = 2
D_EXPERT = D_MODEL // 8
MOE_BLOCK = 256
NORM_EPS = 1e-6
F32 = jnp.float32
BF16 = jnp.bfloat16

VMEM_LIMIT_BYTES = 56 * 1024 * 1024
LANES = 128
SUBLANES = 8
MM_TM = 1024
MM_TN = 512
SEQ_TB = 512

OFF_Z = 0
OFF_XS = OFF_Z + D_B
OFF_RKV = OFF_XS + D_B
OFF_BM = OFF_RKV + 3 * D_A
OFF_CM = OFF_BM + G_B * N_B
OFF_G = OFF_CM + G_B * N_B
OFF_WA = OFF_G + 256
OFF_DT = OFF_WA + LANES
P_W = OFF_DT + LANES


def _block_index(offset, width):
    assert offset % width == 0, (offset, width)
    return offset // width


def _cparams(sem):
    return pltpu.CompilerParams(dimension_semantics=sem, vmem_limit_bytes=VMEM_LIMIT_BYTES)


def _dot(a, b):
    return jnp.dot(a.astype(BF16), b.astype(BF16), preferred_element_type=F32)


def _dot_nt(a, b):
    return lax.dot_general(a.astype(BF16), b.astype(BF16), (((1,), (1,)), ((), ())),
                           preferred_element_type=F32)


def _dot_hilo(x, w):
    hi = x.astype(BF16)
    lo = (x - hi.astype(F32)).astype(BF16)
    return (jnp.dot(hi, w, preferred_element_type=F32) + jnp.dot(lo, w, preferred_element_type=F32))


def _shift_rows(x, prev8, d):
    ext = jnp.concatenate([prev8, x], axis=0)
    return pltpu.roll(ext, d, axis=0)[SUBLANES:]


def _sigmoid(x):
    return 1.0 / (1.0 + jnp.exp(-x))


def _softplus(x):
    return jnp.maximum(x, 0.0) + jnp.log1p(jnp.exp(-jnp.abs(x)))


def _segment_ones(width, seg):
    shift = seg.bit_length() - 1
    r = lax.shift_right_logical(lax.broadcasted_iota(jnp.int32, (width, width), 0), shift)
    c = lax.shift_right_logical(lax.broadcasted_iota(jnp.int32, (width, width), 1), shift)
    return (r == c).astype(BF16)


def _norm_mm_kernel(x_ref, g_ref, w_ref, o_ref, xn_ref):
    @pl.when(pl.program_id(1) == 0)
    def _():
        x = x_ref[...]
        y = x * lax.rsqrt(jnp.mean(x * x, axis=-1, keepdims=True) + NORM_EPS) * g_ref[...]
        xn_ref[...] = y.astype(BF16)

    o_ref[...] = jnp.dot(xn_ref[...], w_ref[...], preferred_element_type=F32)


def norm_matmul(x, g, w):
    m, k = x.shape
    n = w.shape[1]
    return pl.pallas_call(
        _norm_mm_kernel,
        out_shape=jax.ShapeDtypeStruct((m, n), F32),
        grid=(m // MM_TM, n // MM_TN),
        in_specs=[pl.BlockSpec((MM_TM, k), lambda i, j: (i, 0)),
                  pl.BlockSpec((1, k), lambda i, j: (0, 0)),
                  pl.BlockSpec((k, MM_TN), lambda i, j: (0, j))],
        out_specs=pl.BlockSpec((MM_TM, MM_TN), lambda i, j: (i, j)),
        scratch_shapes=[pltpu.VMEM((MM_TM, k), BF16)],
        compiler_params=_cparams(("parallel", "arbitrary")),
        name="norm_matmul",
    )(x, g.reshape(1, k), w)


def _mm_res_kernel(*refs):
    *xw, res_ref, o_ref = refs
    acc = res_ref[...]
    for x_ref, w_ref in zip(xw[0::2], xw[1::2]):
        acc = acc + jnp.dot(x_ref[...], w_ref[...], preferred_element_type=F32)
    o_ref[...] = acc


def matmul_residual(xs, ws, res):
    m, n = res.shape
    in_specs, args = [], []
    for x, w in zip(xs, ws):
        in_specs += [pl.BlockSpec((MM_TM, x.shape[1]), lambda i, j: (i, 0)),
                     pl.BlockSpec((w.shape[0], MM_TN), lambda i, j: (0, j))]
        args += [x, w]
    in_specs.append(pl.BlockSpec((MM_TM, MM_TN), lambda i, j: (i, j)))
    return pl.pallas_call(
        _mm_res_kernel,
        out_shape=jax.ShapeDtypeStruct((m, n), F32),
        grid=(m // MM_TM, n // MM_TN),
        in_specs=in_specs,
        out_specs=pl.BlockSpec((MM_TM, MM_TN), lambda i, j: (i, j)),
        compiler_params=_cparams(("parallel", "arbitrary")),
        name="matmul_residual",
    )(*args, res)


def _norm_router_kernel(x_ref, g_ref, w_ref, xn_ref, lg_ref):
    x = x_ref[...]
    y = (x * lax.rsqrt(jnp.mean(x * x, axis=-1, keepdims=True) + NORM_EPS) * g_ref[...]).astype(BF16)
    xn_ref[...] = y
    lg_ref[...] = jnp.dot(y, w_ref[...], preferred_element_type=F32)


def norm_router(x, g, w):
    m, k = x.shape
    return pl.pallas_call(
        _norm_router_kernel,
        out_shape=(jax.ShapeDtypeStruct((m, k), BF16), jax.ShapeDtypeStruct((m, LANES), F32)),
        grid=(m // MM_TM,),
        in_specs=[pl.BlockSpec((MM_TM, k), lambda i: (i, 0)),
                  pl.BlockSpec((1, k), lambda i: (0, 0)),
                  pl.BlockSpec((k, LANES), lambda i: (0, 0))],
        out_specs=(pl.BlockSpec((MM_TM, k), lambda i: (i, 0)),
                   pl.BlockSpec((MM_TM, LANES), lambda i: (i, 0))),
        compiler_params=_cparams(("parallel",)),
        name="norm_router",
    )(x, g.reshape(1, k), w)


def _final_norm_kernel(x_ref, g_ref, o_ref):
    x = x_ref[...]
    o_ref[...] = x * lax.rsqrt(jnp.mean(x * x, axis=-1, keepdims=True) + NORM_EPS) * g_ref[...]


def final_norm(x, g):
    m, k = x.shape
    return pl.pallas_call(
        _final_norm_kernel,
        out_shape=jax.ShapeDtypeStruct((m, k), F32),
        grid=(m // MM_TM,),
        in_specs=[pl.BlockSpec((MM_TM, k), lambda i: (i, 0)), pl.BlockSpec((1, k), lambda i: (0, 0))],
        out_specs=pl.BlockSpec((MM_TM, k), lambda i: (i, 0)),
        compiler_params=_cparams(("parallel",)),
        name="final_norm",
    )(x, g.reshape(1, k))


RW_C = 64
RW_LANES = 256
RW_UNROLL = 8
RW_PREP = 512
(RV_MU_R, RV_MU_K, RV_MU_V, RV_W0, RV_A0, RV_KK, RV_KA, RV_LNW, RV_LNB, RV_RK, RV_V0) = range(11)
RV_ROWS = 16
EXP_M05 = 0.6065306597126334


def _rwkv_kernel(*refs, nc, vres):
    if vres:
        (pr_ref, pk_ref, pv_ref, pg_ref, pwa_ref, pdv_ref, vf_ref, vec_ref, muwa_ref, mug_ref, mudv_ref,
         wup_ref, aup_ref, gup_ref, vup_ref, ya_ref,
         st_ref, rm_ref, yn_ref, gc_ref, c_r, c_k, c_v, c_g, c_wa, c_dv,
         s_r, s_lw, s_k, s_v, s_kk, s_a, s_gate, s_bonus, s_y) = refs
        vfo_ref = None
    else:
        (pr_ref, pk_ref, pv_ref, pg_ref, pwa_ref, vec_ref, muwa_ref, mug_ref,
         wup_ref, aup_ref, gup_ref, ya_ref, vfo_ref,
         st_ref, rm_ref, yn_ref, gc_ref, c_r, c_k, c_v, c_g, c_wa,
         s_r, s_lw, s_k, s_v, s_kk, s_a, s_gate, s_bonus, s_y) = refs
    c_len, n = RW_C, N_A
    hp = RW_LANES // n
    tb = nc * c_len
    raws = [(pr_ref, c_r), (pk_ref, c_k), (pv_ref, c_v), (pg_ref, c_g), (pwa_ref, c_wa)]
    if vres:
        raws.append((pdv_ref, c_dv))

    @pl.when(pl.program_id(2) == 0)
    def _():
        st_ref[...] = jnp.zeros_like(st_ref)
        for _, c_ref in raws:
            c_ref[...] = jnp.zeros_like(c_ref)

    seg_ones = _segment_ones(RW_LANES, n)
    vec = vec_ref[...]
    row = lambda i: vec[i:i + 1, :]

    def prep0(i, carry):
        t0 = pl.multiple_of(i * RW_PREP, RW_PREP)
        sl = pl.ds(t0, RW_PREP)
        tp = pl.multiple_of(jnp.maximum(t0 - SUBLANES, 0), SUBLANES)

        def mixed(p_ref, c_ref, mu):
            x = p_ref[0, sl, :]
            prev8 = jnp.where(i == 0, c_ref[...], p_ref[0, pl.ds(tp, SUBLANES), :])
            return x + mu * (_shift_rows(x, prev8, 1) - x)

        r = mixed(pr_ref, c_r, row(RV_MU_R))
        k = mixed(pk_ref, c_k, row(RV_MU_K))
        v = mixed(pv_ref, c_v, row(RV_MU_V))
        dg = mixed(pg_ref, c_g, mug_ref[...])
        wa = mixed(pwa_ref, c_wa, muwa_ref[...])
        s_lw[sl, :] = -EXP_M05 * _sigmoid(row(RV_W0) + _dot(jnp.tanh(wa), wup_ref[...]))
        a = _sigmoid(row(RV_A0) + _dot(wa, aup_ref[...]))
        s_a[sl, :] = a
        s_gate[sl, :] = _dot(_sigmoid(dg), gup_ref[...])
        if vres:
            dv = mixed(pdv_ref, c_dv, mudv_ref[...])
            v = v + (vf_ref[0, sl, :] - v) * _sigmoid(row(RV_V0) + _dot(dv, vup_ref[...]))
        else:
            vfo_ref[0, sl, :] = v
        kk = k * row(RV_KK)
        norm = jnp.sqrt(_dot_hilo(kk * kk, seg_ones))
        s_kk[sl, :] = kk / jnp.maximum(norm, 1e-12)
        k = k * (1.0 + (a - 1.0) * row(RV_KA))
        s_r[sl, :] = r
        s_k[sl, :] = k
        s_v[sl, :] = v
        s_bonus[sl, :] = _dot_hilo(r * k * row(RV_RK), seg_ones) * v
        return carry

    lax.fori_loop(0, tb // RW_PREP, prep0, 0)
    for p_ref, c_ref in raws:
        c_ref[...] = p_ref[0, tb - SUBLANES:tb, :]

    rowi = lax.broadcasted_iota(jnp.int32, (c_len, c_len), 0)
    coli = lax.broadcasted_iota(jnp.int32, (c_len, c_len), 1)
    tri_incl = (rowi >= coli).astype(BF16)
    eye = (rowi == coli).astype(F32)
    row2 = lax.broadcasted_iota(jnp.int32, (2 * c_len, 2 * c_len), 0)
    col2 = lax.broadcasted_iota(jnp.int32, (2 * c_len, 2 * c_len), 1)
    rr = jnp.where(row2 >= c_len, row2 - c_len, row2)
    cc = jnp.where(col2 >= c_len, col2 - c_len, col2)
    gmask = (rr - cc) >= jnp.where(row2 >= c_len, 0, 1)

    def prep(c):
        t0 = pl.multiple_of(c * c_len, c_len)
        sl = pl.ds(t0, c_len)
        lw = s_lw[sl, :]
        r = s_r[sl, :]
        k = s_k[sl, :]
        v = s_v[sl, :]
        kk = s_kk[sl, :]
        a = s_a[sl, :]
        lw_hi = lw.astype(BF16)
        lw_lo = (lw - lw_hi.astype(F32)).astype(BF16)
        cum2 = jnp.dot(tri_incl, jnp.concatenate([lw_hi, lw_lo], axis=1), preferred_element_type=F32)
        cum = cum2[:, :RW_LANES] + cum2[:, RW_LANES:]
        cum_prev = cum - lw
        cum_end = cum[c_len - 1:c_len, :]
        g_inv = jnp.exp(-cum)
        to_end = jnp.exp(cum_end - cum)
        beta = kk * a
        al_h = -kk * jnp.exp(cum_prev)
        r_h = r * jnp.exp(cum)
        be_c = beta * g_inv
        k_c = k * g_inv
        be_t = (beta * to_end).T
        k_t = (k * to_end).T
        gc_ref[c] = jnp.exp(cum.T[:, c_len - 1:c_len])
        probs = []
        for h in range(hp):
            hs = slice(h * n, (h + 1) * n)
            probs.append(dict(c=c, h=h, ah=al_h[:, hs], rh=r_h[:, hs], vh=v[:, hs],
                              bc=be_c[:, hs], kc=k_c[:, hs], bt=be_t[hs, :], kt=k_t[hs, :]))
        return probs

    def phase1(i, carry):
        probs = []
        for u in range(RW_UNROLL):
            probs += prep(i * RW_UNROLL + u)
        for q in probs:
            g = _dot_nt(jnp.concatenate([q["ah"], q["rh"]], axis=0),
                        jnp.concatenate([q["bc"], q["kc"]], axis=0))
            g = jnp.where(gmask, g, 0.0)
            q["a_ak"] = g[:c_len, c_len:]
            q["lower"] = g[c_len:, :]
            q["p"] = g[:c_len, :c_len]
            q["t"] = eye + q["p"]
        for _ in range(5):
            for q in probs:
                q["p"] = _dot(q["p"], q["p"])
            for q in probs:
                q["t"] = q["t"] + _dot(q["t"], q["p"])
        for q in probs:
            q["av"] = _dot(q["a_ak"], q["vh"])
        for q in probs:
            q["aw"] = _dot(q["t"], jnp.concatenate([q["ah"], q["av"]], axis=1))
        for q in probs:
            z = jnp.concatenate(
                [q["aw"], jnp.concatenate([jnp.zeros((c_len, n), F32), q["vh"]], axis=1)], axis=0)
            lhs = jnp.concatenate(
                [q["lower"], jnp.concatenate([q["bt"], q["kt"]], axis=1)], axis=0)
            f = _dot(lhs, z)
            rm = f[:, :n] + jnp.concatenate([q["rh"], jnp.zeros((n, n), F32)], axis=0)
            rm_ref[q["c"], q["h"]] = rm.astype(BF16)
            yn_ref[q["c"], q["h"]] = f[:, n:]
        return carry

    lax.fori_loop(0, nc // RW_UNROLL, phase1, 0)

    def phase2(c, carry):
        t0 = pl.multiple_of(c * c_len, c_len)
        gcol = gc_ref[c]
        ss = [st_ref[h] for h in range(hp)]
        fs = [jnp.dot(rm_ref[c, h], ss[h].astype(BF16), preferred_element_type=F32)
              for h in range(hp)]
        ys = []
        for h in range(hp):
            yn = yn_ref[c, h]
            ys.append(fs[h][:c_len] + yn[:c_len])
            st_ref[h] = gcol[h * n:(h + 1) * n] * ss[h] + fs[h][c_len:] + yn[c_len:]
        s_y[pl.ds(t0, c_len), :] = jnp.concatenate(ys, axis=1)
        return carry

    lax.fori_loop(0, nc, phase2, 0)

    def post(i, carry):
        t0 = pl.multiple_of(i * RW_PREP, RW_PREP)
        sl = pl.ds(t0, RW_PREP)
        y = s_y[sl, :]
        mean = _dot_hilo(y, seg_ones) * (1.0 / n)
        yc = y - mean
        var = _dot_hilo(yc * yc, seg_ones) * (1.0 / n)
        yn = yc * lax.rsqrt(var + RWKV_GN_EPS) * row(RV_LNW) + row(RV_LNB)
        ya_ref[0, sl, :] = ((yn + s_bonus[sl, :]) * s_gate[sl, :]).astype(BF16)
        return carry

    lax.fori_loop(0, tb // RW_PREP, post, 0)


def rwkv7_mixer(p, mu, w_up, a_up, g_up, vecs, r_k, v_first, vres_params):
    bsz, seq, _ = p.shape
    vres = vres_params is not None
    tb = SEQ_TB
    nc = tb // RW_C
    hp = RW_LANES // N_A
    pad_rows = lambda w, before, total: jnp.pad(w, ((before, total - before - w.shape[0]), (0, 0))).astype(BF16)
    rows = [mu[:D_A], mu[D_A:2 * D_A], mu[2 * D_A:3 * D_A]] + [vecs[i] for i in range(6)] + [r_k.reshape(D_A)]
    rows.append(vres_params[2] if vres else jnp.zeros((D_A,), F32))
    table = jnp.pad(jnp.stack(rows), ((0, RV_ROWS - len(rows)), (0, 0)))
    mu_wa = mu[3 * D_A:3 * D_A + R_W + R_A].reshape(1, LANES)
    mu_g = jnp.pad(mu[3 * D_A + R_W + R_A:], (0, 256 - R_G)).reshape(1, 256)
    blk = lambda w, f: pl.BlockSpec((1, tb, w), f)
    col = lambda w: pl.BlockSpec((w.shape[0], RW_LANES), lambda b, h, t: (0, h))
    full = lambda w: pl.BlockSpec(w.shape, lambda b, h, t: (0, 0))
    q = RW_LANES
    wup = pad_rows(w_up, 0, LANES)
    aup = pad_rows(a_up, R_W, LANES)
    gup = pad_rows(g_up, 0, 256)
    in_specs = [blk(q, lambda b, h, t: (b, t, _block_index(OFF_RKV, q) + h)),
                blk(q, lambda b, h, t: (b, t, _block_index(OFF_RKV + D_A, q) + h)),
                blk(q, lambda b, h, t: (b, t, _block_index(OFF_RKV + 2 * D_A, q) + h)),
                blk(256, lambda b, h, t: (b, t, _block_index(OFF_G, 256))),
                blk(LANES, lambda b, h, t: (b, t, _block_index(OFF_WA, LANES)))]
    args = [p, p, p, p, p]
    if vres:
        mu_dv = jnp.pad(vres_params[0], (H_B, LANES - H_B - R_V)).reshape(1, LANES)
        vup = pad_rows(vres_params[1], H_B, LANES)
        in_specs += [blk(LANES, lambda b, h, t: (b, t, _block_index(OFF_DT, LANES))),
                     blk(q, lambda b, h, t: (b, t, h))]
        args += [p, v_first]
    in_specs += [col(table), full(mu_wa), full(mu_g)]
    args += [table, mu_wa, mu_g]
    if vres:
        in_specs.append(full(mu_dv))
        args.append(mu_dv)
    in_specs += [col(wup), col(aup), col(gup)]
    args += [wup, aup, gup]
    if vres:
        in_specs.append(col(vup))
        args.append(vup)
    out_blk = blk(q, lambda b, h, t: (b, t, h))
    out_shape = [jax.ShapeDtypeStruct((bsz, seq, D_A), BF16)]
    out_specs = [out_blk]
    if not vres:
        out_shape.append(jax.ShapeDtypeStruct((bsz, seq, D_A), F32))
        out_specs.append(out_blk)
    carries = [pltpu.VMEM((SUBLANES, w), F32) for w in (q, q, q, 256, LANES) + ((LANES,) if vres else ())]
    outs = pl.pallas_call(
        functools.partial(_rwkv_kernel, nc=nc, vres=vres),
        out_shape=tuple(out_shape),
        grid=(bsz, D_A // q, seq // tb),
        in_specs=in_specs,
        out_specs=tuple(out_specs),
        scratch_shapes=[
            pltpu.VMEM((hp, N_A, N_A), F32),
            pltpu.VMEM((nc, hp, RW_C + N_A, N_A), BF16),
            pltpu.VMEM((nc, hp, RW_C + N_A, N_A), F32),
            pltpu.VMEM((nc, q, 1), F32),
        ] + carries + [pltpu.VMEM((tb, q), F32)] * 9,
        compiler_params=_cparams(("parallel", "parallel", "arbitrary")),
        name="rwkv7_mixer",
    )(*args)
    return (outs[0], v_first) if vres else (outs[0], outs[1])


SSD_TB = 256


def _ssd_kernel(z_ref, xs_ref, bm_ref, cm_ref, dt_ref, cw_ref, cb_ref, hd_ref, dsk_ref, ng_ref, o_ref,
                st_ref, c_xs, c_bm, c_cm, s_xs, s_bm, s_cm, s_dt, s_la):
    tb = SSD_TB
    raws = [(xs_ref, c_xs, 0, D_B), (bm_ref, c_bm, D_B, G_B * N_B), (cm_ref, c_cm, D_B + G_B * N_B, G_B * N_B)]

    @pl.when(pl.program_id(1) == 0)
    def _():
        st_ref[...] = jnp.zeros_like(st_ref)
        for _, c_ref, _, _ in raws:
            c_ref[...] = jnp.zeros_like(c_ref)

    for (p_ref, c_ref, off, width), s_ref in zip(raws, (s_xs, s_bm, s_cm)):
        x = p_ref[0]
        prev8 = c_ref[...]
        acc = cb_ref[:, off:off + width] + cw_ref[CONV_K - 1:CONV_K, off:off + width] * x
        for d in range(1, CONV_K):
            acc = acc + cw_ref[CONV_K - 1 - d:CONV_K - d, off:off + width] * _shift_rows(x, prev8, d)
        s_ref[...] = acc * _sigmoid(acc)
        c_ref[...] = p_ref[0, tb - SUBLANES:tb, :]
    dt = _softplus(dt_ref[0] + hd_ref[0:1, :])
    s_dt[...] = dt
    s_la[...] = dt * (-jnp.exp(hd_ref[1:2, :]))

    rowi = lax.broadcasted_iota(jnp.int32, (CHUNK, CHUNK), 0)
    coli = lax.broadcasted_iota(jnp.int32, (CHUNK, CHUNK), 1)
    causal = rowi >= coli
    tri_incl = causal.astype(BF16)
    first_head = coli < P_B
    expand = (lax.shift_right_logical(lax.broadcasted_iota(jnp.int32, (LANES, D_B), 1), P_B.bit_length() - 1)
              == lax.broadcasted_iota(jnp.int32, (LANES, D_B), 0)).astype(BF16)
    pw = 2 * P_B

    def chunk(c, carry):
        t0 = pl.multiple_of(c * CHUNK, CHUNK)
        sl = pl.ds(t0, CHUNK)
        la = s_la[sl, :]
        la_hi = la.astype(BF16)
        la_lo = (la - la_hi.astype(F32)).astype(BF16)
        cum2 = jnp.dot(tri_incl, jnp.concatenate([la_hi, la_lo], axis=1), preferred_element_type=F32)
        cum = cum2[:, :LANES] + cum2[:, LANES:]
        cum_t = cum.T
        ecum_f = _dot_hilo(jnp.exp(cum), expand)
        dend_f = _dot_hilo(jnp.exp(cum[CHUNK - 1:CHUNK, :] - cum), expand)
        xdt_f = s_xs[sl, :] * _dot_hilo(s_dt[sl, :], expand)
        xde_f = xdt_f * dend_f
        outs = []
        for g in range(G_B):
            cmg = s_cm[sl, g * N_B:(g + 1) * N_B]
            bmg = s_bm[sl, g * N_B:(g + 1) * N_B]
            cb = _dot_nt(cmg, bmg)
            bmt = bmg.T
            pairs = [g * (J_B // 2) + m for m in range(J_B // 2)]
            lanes = [slice(q * pw, (q + 1) * pw) for q in pairs]
            sc = [[cb * jnp.exp(jnp.where(causal, cum[:, h:h + 1] - cum_t[h:h + 1, :], -1e30))
                   for h in (2 * q, 2 * q + 1)] for q in pairs]
            sts = [st_ref[q] for q in pairs]
            y_in = [_dot(jnp.concatenate(s2, axis=1),
                         jnp.concatenate([jnp.where(first_head, xdt_f[:, ln], 0.0),
                                          jnp.where(first_head, 0.0, xdt_f[:, ln])], axis=0))
                    for s2, ln in zip(sc, lanes)]
            y_st = [_dot(cmg, st) for st in sts]
            upd = [_dot(bmt, xde_f[:, ln]) for ln in lanes]
            for j, (q, ln) in enumerate(zip(pairs, lanes)):
                st_ref[q] = sts[j] * ecum_f[CHUNK - 1:CHUNK, ln] + upd[j]
                outs.append(y_in[j] + y_st[j] * ecum_f[:, ln])
        y = jnp.concatenate(outs, axis=1)
        zc = z_ref[0, sl, :]
        y = (y + dsk_ref[...] * s_xs[sl, :]) * (zc * _sigmoid(zc))
        pieces = []
        gw = D_B // G_B
        for g in range(G_B):
            yg = y[:, g * gw:(g + 1) * gw]
            pieces.append(yg * lax.rsqrt(jnp.mean(yg * yg, axis=-1, keepdims=True) + MAMBA_NORM_EPS))
        o_ref[0, sl, :] = (jnp.concatenate(pieces, axis=1) * ng_ref[...]).astype(BF16)
        return carry

    lax.fori_loop(0, tb // CHUNK, chunk, 0)


def mamba2_mixer(p, conv_w, conv_b, head, norm_g):
    bsz, seq, _ = p.shape
    tb = SSD_TB
    hd = jnp.pad(head[:2], ((0, SUBLANES - 2), (0, LANES - H_B)))
    dsk = jnp.repeat(head[2], P_B).reshape(1, D_B)
    gw = G_B * N_B
    blk = lambda w, off: pl.BlockSpec((1, tb, w), lambda b, t: (b, t, _block_index(off, w)))
    full = lambda a: pl.BlockSpec(a.shape, lambda b, t: (0,) * a.ndim)
    cb = conv_b.reshape(1, C_CONV)
    ng = norm_g.reshape(1, D_B)
    return pl.pallas_call(
        _ssd_kernel,
        out_shape=jax.ShapeDtypeStruct((bsz, seq, D_B), BF16),
        grid=(bsz, seq // tb),
        in_specs=[blk(D_B, OFF_Z), blk(D_B, OFF_XS), blk(gw, OFF_BM), blk(gw, OFF_CM), blk(LANES, OFF_DT),
                  full(conv_w), full(cb), full(hd), full(dsk), full(ng)],
        out_specs=pl.BlockSpec((1, tb, D_B), lambda b, t: (b, t, 0)),
        scratch_shapes=[
            pltpu.VMEM((H_B // 2, N_B, 2 * P_B), F32),
            pltpu.VMEM((SUBLANES, D_B), F32), pltpu.VMEM((SUBLANES, gw), F32), pltpu.VMEM((SUBLANES, gw), F32),
            pltpu.VMEM((tb, D_B), F32), pltpu.VMEM((tb, gw), F32), pltpu.VMEM((tb, gw), F32),
            pltpu.VMEM((tb, LANES), F32), pltpu.VMEM((tb, LANES), F32),
        ],
        compiler_params=_cparams(("parallel", "arbitrary")),
        name="mamba2_mixer",
    )(p, p, p, p, p, conv_w, cb, hd, dsk, ng)


def _lru_kernel(gate_ref, xr_ref, cw_ref, cb_ref, gw_ref, gb_ref, lam_ref, o_ref, c_x, h_ref):
    tb = SEQ_TB
    first = pl.program_id(2) == 0

    @pl.when(first)
    def _():
        c_x[...] = jnp.zeros_like(c_x)
        h_ref[...] = jnp.zeros_like(h_ref)

    x = xr_ref[0]
    prev8 = c_x[...]
    xb = cb_ref[...] + cw_ref[CONV_K - 1:CONV_K, :] * x
    for d in range(1, CONV_K):
        xb = xb + cw_ref[CONV_K - 1 - d:CONV_K - d, :] * _shift_rows(x, prev8, d)
    c_x[...] = xr_ref[0, tb - SUBLANES:tb, :]
    gates = _sigmoid(_dot(xb, gw_ref[0]) + gb_ref[0])
    r_gate, i_gate = gates[:, :BS_C], gates[:, BS_C:]
    log_a = -C_RG * r_gate * _softplus(-lam_ref[...])
    a = jnp.exp(log_a)
    mult = jnp.sqrt(-jnp.tanh(log_a) * (a * a + 1.0))
    rowi = lax.broadcasted_iota(jnp.int32, (tb, BS_C), 0)
    mult = jnp.where(jnp.logical_and(first, rowi == 0), 1.0, mult)
    b = mult * (i_gate * xb)
    d = 1
    while d < tb:
        if d < SUBLANES:
            a_s = jnp.where(rowi >= d, pltpu.roll(a, d, axis=0), 1.0)
            b_s = jnp.where(rowi >= d, pltpu.roll(b, d, axis=0), 0.0)
        else:
            a_s = jnp.concatenate([jnp.ones((d, BS_C), F32), a[:tb - d]], axis=0)
            b_s = jnp.concatenate([jnp.zeros((d, BS_C), F32), b[:tb - d]], axis=0)
        b = b + a * b_s
        a = a * a_s
        d *= 2
    h = b + a * h_ref[...]
    h_ref[...] = h[tb - 1:tb, :]
    g = gate_ref[0]
    gelu = 0.5 * g * (1.0 + jnp.tanh(0.7978845608028654 * (g + 0.044715 * (g * g * g))))
    o_ref[0] = (h * gelu).astype(BF16)


def rglru_mixer(p, conv_w, conv_b, gate_w, gate_b, lam):
    bsz, seq, _ = p.shape
    tb = SEQ_TB
    gw = jnp.concatenate([gate_w[0], gate_w[1]], axis=-1).astype(BF16)
    gb = jnp.concatenate([gate_b[0].reshape(H_C, 1, BS_C), gate_b[1].reshape(H_C, 1, BS_C)], axis=-1)
    vec = lambda a: pl.BlockSpec((a.shape[0], BS_C), lambda b, h, t: (0, h))
    cb = conv_b.reshape(1, D_RNN)
    lm = lam.reshape(1, D_RNN)
    return pl.pallas_call(
        _lru_kernel,
        out_shape=jax.ShapeDtypeStruct((bsz, seq, D_RNN), BF16),
        grid=(bsz, H_C, seq // tb),
        in_specs=[pl.BlockSpec((1, tb, BS_C), lambda b, h, t: (b, t, h)),
                  pl.BlockSpec((1, tb, BS_C), lambda b, h, t: (b, t, H_C + h)),
                  vec(conv_w), vec(cb),
                  pl.BlockSpec((1, BS_C, 2 * BS_C), lambda b, h, t: (h, 0, 0)),
                  pl.BlockSpec((1, 1, 2 * BS_C), lambda b, h, t: (h, 0, 0)),
                  vec(lm)],
        out_specs=pl.BlockSpec((1, tb, BS_C), lambda b, h, t: (b, t, h)),
        scratch_shapes=[pltpu.VMEM((SUBLANES, BS_C), F32), pltpu.VMEM((1, BS_C), F32)],
        compiler_params=_cparams(("parallel", "parallel", "arbitrary")),
        name="rglru_mixer",
    )(p, p, conv_w, cb, gw, gb, lm)


def _moe_ffn_kernel(be_ref, x_ref, wg_ref, wu_ref, wd_ref, o_ref, wgu_s, wd_s):
    i = pl.program_id(0)

    @pl.when(jnp.logical_or(i == 0, be_ref[i] != be_ref[jnp.maximum(i - 1, 0)]))
    def _():
        wgu_s[:, :D_EXPERT] = wg_ref[0, 0].astype(BF16)
        wgu_s[:, D_EXPERT:] = wu_ref[0, 0].astype(BF16)
        wd_s[...] = wd_ref[0, 0].astype(BF16)

    gu = jnp.dot(x_ref[...], wgu_s[...], preferred_element_type=F32)
    g, u = gu[:, :D_EXPERT], gu[:, D_EXPERT:]
    hb = (g * jax.nn.sigmoid(g)) * u
    o_ref[...] = jnp.dot(hb.astype(BF16), wd_s[...], preferred_element_type=F32).astype(BF16)


def moe_ffn(block_exp, xg, w_gate, w_up, w_down, layer):
    n_slots, d = xg.shape
    grid_spec = pltpu.PrefetchScalarGridSpec(
        num_scalar_prefetch=1,
        grid=(n_slots // MOE_BLOCK,),
        in_specs=[pl.BlockSpec((MOE_BLOCK, d), lambda i, be: (i, 0)),
                  pl.BlockSpec((1, 1, d, D_EXPERT), lambda i, be: (layer, be[i], 0, 0)),
                  pl.BlockSpec((1, 1, d, D_EXPERT), lambda i, be: (layer, be[i], 0, 0)),
                  pl.BlockSpec((1, 1, D_EXPERT, d), lambda i, be: (layer, be[i], 0, 0))],
        out_specs=pl.BlockSpec((MOE_BLOCK, d), lambda i, be: (i, 0)),
        scratch_shapes=[pltpu.VMEM((d, 2 * D_EXPERT), BF16), pltpu.VMEM((D_EXPERT, d), BF16)],
    )
    return pl.pallas_call(
        _moe_ffn_kernel,
        out_shape=jax.ShapeDtypeStruct((n_slots, d), BF16),
        grid_spec=grid_spec,
        compiler_params=_cparams(("arbitrary",)),
        name="moe_ffn",
    )(block_exp, xg, w_gate, w_up, w_down)


def hier_moe(h, norm_g, w_grp, b_grp, w_exp, b_exp, w_gate, w_up, w_down, layer):
    t, d = h.shape
    w_r = jnp.pad(jnp.concatenate([w_grp, w_exp], axis=1), ((0, 0), (0, LANES - N_GROUPS_E - N_EXPERTS)))
    xn, logits = norm_router(h, norm_g, w_r.astype(BF16))
    g_logits = logits[:, :N_GROUPS_E] + b_grp.astype(F32)
    g_idx = jnp.argmax(g_logits, axis=-1)
    g_w = jnp.take_along_axis(jax.nn.softmax(g_logits, axis=-1), g_idx[:, None], axis=-1)
    e_logits = (logits[:, N_GROUPS_E:N_GROUPS_E + N_EXPERTS] + b_exp.astype(F32)).reshape(
        t, N_GROUPS_E, E_PER_GROUP)
    e_logits = jnp.take_along_axis(e_logits, g_idx[:, None, None], axis=1)[:, 0]
    e_top, e_loc = lax.top_k(e_logits, TOP_K_E)
    e_w = jax.nn.softmax(e_top, axis=-1) * g_w
    e_flat = (g_idx[:, None] * E_PER_GROUP + e_loc).reshape(-1).astype(jnp.int32)
    n_assign = t * TOP_K_E
    onehot = (e_flat[:, None] == jnp.arange(N_EXPERTS, dtype=jnp.int32)[None, :]).astype(jnp.int32)
    running = jnp.cumsum(onehot, axis=0)
    counts = running[-1]
    rank = jnp.take_along_axis(running, e_flat[:, None], axis=1)[:, 0] - 1
    padded = (counts + MOE_BLOCK - 1) // MOE_BLOCK * MOE_BLOCK
    raw_start = jnp.cumsum(counts) - counts
    pad_start = jnp.cumsum(padded) - padded
    dest = pad_start[e_flat] + rank
    n_slots = n_assign + N_EXPERTS * MOE_BLOCK
    n_blocks = n_slots // MOE_BLOCK
    block_start = jnp.arange(n_blocks, dtype=jnp.int32) * MOE_BLOCK
    block_exp = jnp.minimum(
        jnp.sum(block_start[:, None] >= (pad_start + padded)[None, :], axis=-1), N_EXPERTS - 1)
    order = jnp.argsort(e_flat)
    off = (block_start - pad_start[block_exp])[:, None] + jnp.arange(MOE_BLOCK, dtype=jnp.int32)[None, :]
    src = order[jnp.clip(raw_start[block_exp][:, None] + off, 0, n_assign - 1)]
    spread = jnp.arange(n_slots, dtype=jnp.int32).reshape(n_blocks, MOE_BLOCK) % t
    slot_tok = jnp.where(off < counts[block_exp][:, None], src // TOP_K_E, spread).reshape(n_slots)
    y = moe_ffn(block_exp.astype(jnp.int32), xn[slot_tok], w_gate, w_up, w_down, layer)
    slot_of = dest.reshape(t, TOP_K_E)
    return h + (y[slot_of[:, 0]] * e_w[:, 0:1] + y[slot_of[:, 1]] * e_w[:, 1:2])


def _hybrid_w_in(w_in):
    d = w_in.shape[0]
    zeros = lambda n: jnp.zeros((d, n), w_in.dtype)
    c_wa, c_g = 3 * D_A, 3 * D_A + R_W + R_A
    dv = w_in[:, C_HY:C_HY + R_V] if w_in.shape[1] > C_HY else zeros(R_V)
    c_x = C_A + D_B
    return jnp.concatenate([
        w_in[:, C_A:c_x], w_in[:, c_x:c_x + D_B], w_in[:, :3 * D_A], w_in[:, c_x + D_B:c_x + C_CONV],
        w_in[:, c_g:C_A], zeros(256 - R_G), w_in[:, c_wa:c_g],
        w_in[:, c_x + C_CONV:C_HY], dv, zeros(LANES - H_B - R_V)], axis=1).astype(BF16)


def kernel(x, norm_mix, norm_ffn, norm_final, hy_w_in_first, hy_w_in_vres, rk_mu, rk_mu_v, rk_w_up, rk_a_up, rk_g_up, rk_vecs, rk_r_k, rk_v_up, rk_v0, mb_conv_w, mb_conv_b, mb_head, mb_norm, hy_w_out, lr_w_in, lr_conv_w, lr_conv_b, lr_gate_w, lr_gate_b, lr_lambda, lr_w_out, moe_w_grp, moe_b_grp, moe_w_exp, moe_b_exp, moe_w_gate, moe_w_up, moe_w_down):
    bsz, seq, d = x.shape
    t = bsz * seq
    h = x.reshape(t, d)
    v_first = None
    for layer in range(DEPTH):
        j = layer // 2
        if layer % 2 == 0:
            w_in = hy_w_in_first if j == 0 else hy_w_in_vres[j - 1]
            vres_params = None if j == 0 else (rk_mu_v[j - 1], rk_v_up[j - 1], rk_v0[j - 1])
            p = norm_matmul(h, norm_mix[layer], _hybrid_w_in(w_in)).reshape(bsz, seq, P_W)
            ya, v_first = rwkv7_mixer(p, rk_mu[j], rk_w_up[j], rk_a_up[j], rk_g_up[j], rk_vecs[j], rk_r_k[j],
                                      v_first, vres_params)
            yb = mamba2_mixer(p, mb_conv_w[j], mb_conv_b[j], mb_head[j], mb_norm[j])
            w_out = hy_w_out[j].astype(BF16)
            h = matmul_residual([ya.reshape(t, D_A), yb.reshape(t, D_B)], [w_out[:D_A], w_out[D_A:]], h)
        else:
            p = norm_matmul(h, norm_mix[layer], lr_w_in[j].astype(BF16)).reshape(bsz, seq, 2 * D_RNN)
            y = rglru_mixer(p, lr_conv_w[j], lr_conv_b[j], lr_gate_w[j], lr_gate_b[j], lr_lambda[j])
            h = matmul_residual([y.reshape(t, D_RNN)], [lr_w_out[j].astype(BF16)], h)
        h = hier_moe(h, norm_ffn[layer], moe_w_grp[layer], moe_b_grp[layer], moe_w_exp[layer],
                     moe_b_exp[layer], moe_w_gate, moe_w_up, moe_w_down, layer)
    return final_norm(h, norm_final).reshape(bsz, seq, d)
```

```python
import functools

import jax
import jax.numpy as jnp
from jax import lax
from jax.experimental import pallas as pl
from jax.experimental.pallas import tpu as pltpu

D_MODEL = 2048
DEPTH = 4
D_A = D_MODEL // 2
N_A = 64
H_A = D_A // N_A
R_W = 64
R_A = 64
R_V = 32
R_G = 160
RWKV_GN_EPS = 64e-5
C_A = 3 * D_A + R_W + R_A + R_G
D_B = D_MODEL
P_B = 64
H_B = D_B // P_B
N_B = 128
G_B = 4
J_B = H_B // G_B
CONV_K = 4
C_CONV = D_B + 2 * G_B * N_B
C_B = D_B + C_CONV + H_B
CHUNK = 128
MAMBA_NORM_EPS = 1e-5
C_HY = C_A + C_B
D_CAT = D_A + D_B
BS_C = 256
D_RNN = ((4 * D_MODEL // 3 + BS_C // 2) // BS_C) * BS_C
H_C = D_RNN // BS_C
C_RG = 8.0
N_GROUPS_E = 8
E_PER_GROUP = 8
N_EXPERTS = N_GROUPS_E * E_PER_GROUP
TOP_K_E = 2
D_EXPERT = D_MODEL // 8
MOE_BLOCK = 256
NORM_EPS = 1e-6
F32 = jnp.float32
BF16 = jnp.bfloat16

VMEM_LIMIT_BYTES = 56 * 1024 * 1024
LANES = 128
SUBLANES = 8
MM_TM = 1024
MM_TN = 512
SEQ_TB = 512

OFF_Z = 0
OFF_XS = OFF_Z + D_B
OFF_RKV = OFF_XS + D_B
OFF_BM = OFF_RKV + 3 * D_A
OFF_CM = OFF_BM + G_B * N_B
OFF_G = OFF_CM + G_B * N_B
OFF_WA = OFF_G + 256
OFF_DT = OFF_WA + LANES
P_W = OFF_DT + LANES


def _block_index(offset, width):
    assert offset % width == 0, (offset, width)
    return offset // width


def _cparams(sem):
    return pltpu.CompilerParams(dimension_semantics=sem, vmem_limit_bytes=VMEM_LIMIT_BYTES)


def _dot(a, b):
    return jnp.dot(a.astype(BF16), b.astype(BF16), preferred_element_type=F32)


def _dot_nt(a, b):
    return lax.dot_general(a.astype(BF16), b.astype(BF16), (((1,), (1,)), ((), ())),
                           preferred_element_type=F32)


def _dot_hilo(x, w):
    hi = x.astype(BF16)
    lo = (x - hi.astype(F32)).astype(BF16)
    return (jnp.dot(hi, w, preferred_element_type=F32) + jnp.dot(lo, w, preferred_element_type=F32))


def _shift_rows(x, prev8, d):
    ext = jnp.concatenate([prev8, x], axis=0)
    return pltpu.roll(ext, d, axis=0)[SUBLANES:]


def _sigmoid(x):
    return 1.0 / (1.0 + jnp.exp(-x))


def _softplus(x):
    return jnp.maximum(x, 0.0) + jnp.log1p(jnp.exp(-jnp.abs(x)))


def _segment_ones(width, seg):
    shift = seg.bit_length() - 1
    r = lax.shift_right_logical(lax.broadcasted_iota(jnp.int32, (width, width), 0), shift)
    c = lax.shift_right_logical(lax.broadcasted_iota(jnp.int32, (width, width), 1), shift)
    return (r == c).astype(BF16)


def _norm_mm_kernel(x_ref, g_ref, w_ref, o_ref, xn_ref):
    @pl.when(pl.program_id(1) == 0)
    def _():
        x = x_ref[...]
        y = x * lax.rsqrt(jnp.mean(x * x, axis=-1, keepdims=True) + NORM_EPS) * g_ref[...]
        xn_ref[...] = y.astype(BF16)

    o_ref[...] = jnp.dot(xn_ref[...], w_ref[...], preferred_element_type=F32)


def norm_matmul(x, g, w):
    m, k = x.shape
    n = w.shape[1]
    return pl.pallas_call(
        _norm_mm_kernel,
        out_shape=jax.ShapeDtypeStruct((m, n), F32),
        grid=(m // MM_TM, n // MM_TN),
        in_specs=[pl.BlockSpec((MM_TM, k), lambda i, j: (i, 0)),
                  pl.BlockSpec((1, k), lambda i, j: (0, 0)),
                  pl.BlockSpec((k, MM_TN), lambda i, j: (0, j))],
        out_specs=pl.BlockSpec((MM_TM, MM_TN), lambda i, j: (i, j)),
        scratch_shapes=[pltpu.VMEM((MM_TM, k), BF16)],
        compiler_params=_cparams(("parallel", "arbitrary")),
        name="norm_matmul",
    )(x, g.reshape(1, k), w)


def _mm_res_kernel(*refs):
    *xw, res_ref, o_ref = refs
    acc = res_ref[...]
    for x_ref, w_ref in zip(xw[0::2], xw[1::2]):
        acc = acc + jnp.dot(x_ref[...], w_ref[...], preferred_element_type=F32)
    o_ref[...] = acc


def matmul_residual(xs, ws, res):
    m, n = res.shape
    in_specs, args = [], []
    for x, w in zip(xs, ws):
        in_specs += [pl.BlockSpec((MM_TM, x.shape[1]), lambda i, j: (i, 0)),
                     pl.BlockSpec((w.shape[0], MM_TN), lambda i, j: (0, j))]
        args += [x, w]
    in_specs.append(pl.BlockSpec((MM_TM, MM_TN), lambda i, j: (i, j)))
    return pl.pallas_call(
        _mm_res_kernel,
        out_shape=jax.ShapeDtypeStruct((m, n), F32),
        grid=(m // MM_TM, n // MM_TN),
        in_specs=in_specs,
        out_specs=pl.BlockSpec((MM_TM, MM_TN), lambda i, j: (i, j)),
        compiler_params=_cparams(("parallel", "arbitrary")),
        name="matmul_residual",
    )(*args, res)


def _norm_router_kernel(x_ref, g_ref, w_ref, xn_ref, lg_ref):
    x = x_ref[...]
    y = (x * lax.rsqrt(jnp.mean(x * x, axis=-1, keepdims=True) + NORM_EPS) * g_ref[...]).astype(BF16)
    xn_ref[...] = y
    lg_ref[...] = jnp.dot(y, w_ref[...], preferred_element_type=F32)


def norm_router(x, g, w):
    m, k = x.shape
    return pl.pallas_call(
        _norm_router_kernel,
        out_shape=(jax.ShapeDtypeStruct((m, k), BF16), jax.ShapeDtypeStruct((m, LANES), F32)),
        grid=(m // MM_TM,),
        in_specs=[pl.BlockSpec((MM_TM, k), lambda i: (i, 0)),
                  pl.BlockSpec((1, k), lambda i: (0, 0)),
                  pl.BlockSpec((k, LANES), lambda i: (0, 0))],
        out_specs=(pl.BlockSpec((MM_TM, k), lambda i: (i, 0)),
                   pl.BlockSpec((MM_TM, LANES), lambda i: (i, 0))),
        compiler_params=_cparams(("parallel",)),
        name="norm_router",
    )(x, g.reshape(1, k), w)


def _final_norm_kernel(x_ref, g_ref, o_ref):
    x = x_ref[...]
    o_ref[...] = x * lax.rsqrt(jnp.mean(x * x, axis=-1, keepdims=True) + NORM_EPS) * g_ref[...]


def final_norm(x, g):
    m, k = x.shape
    return pl.pallas_call(
        _final_norm_kernel,
        out_shape=jax.ShapeDtypeStruct((m, k), F32),
        grid=(m // MM_TM,),
        in_specs=[pl.BlockSpec((MM_TM, k), lambda i: (i, 0)), pl.BlockSpec((1, k), lambda i: (0, 0))],
        out_specs=pl.BlockSpec((MM_TM, k), lambda i: (i, 0)),
        compiler_params=_cparams(("parallel",)),
        name="final_norm",
    )(x, g.reshape(1, k))


RW_C = 64
RW_LANES = 512
RW_UNROLL = 4
RW_PREP = 512
(RV_MU_R, RV_MU_K, RV_MU_V, RV_W0, RV_A0, RV_KK, RV_KA, RV_LNW, RV_LNB, RV_RK, RV_V0) = range(11)
RV_ROWS = 16
EXP_M05 = 0.6065306597126334


def _rwkv_kernel(*refs, nc, vres):
    if vres:
        (pr_ref, pk_ref, pv_ref, pg_ref, pwa_ref, pdv_ref, vf_ref, vec_ref, muwa_ref, mug_ref, mudv_ref,
         wup_ref, aup_ref, gup_ref, vup_ref, ya_ref,
         st_ref, rm_ref, yn_ref, gc_ref, c_r, c_k, c_v, c_g, c_wa, c_dv,
         s_r, s_lw, s_k, s_v, s_kk, s_a, s_gate, s_bonus, s_y) = refs
        vfo_ref = None
    else:
        (pr_ref, pk_ref, pv_ref, pg_ref, pwa_ref, vec_ref, muwa_ref, mug_ref,
         wup_ref, aup_ref, gup_ref, ya_ref, vfo_ref,
         st_ref, rm_ref, yn_ref, gc_ref, c_r, c_k, c_v, c_g, c_wa,
         s_r, s_lw, s_k, s_v, s_kk, s_a, s_gate, s_bonus, s_y) = refs
    c_len, n = RW_C, N_A
    hp = RW_LANES // n
    tb = nc * c_len
    raws = [(pr_ref, c_r), (pk_ref, c_k), (pv_ref, c_v), (pg_ref, c_g), (pwa_ref, c_wa)]
    if vres:
        raws.append((pdv_ref, c_dv))

    @pl.when(pl.program_id(2) == 0)
    def _():
        st_ref[...] = jnp.zeros_like(st_ref)
        for _, c_ref in raws:
            c_ref[...] = jnp.zeros_like(c_ref)

    seg_ones = _segment_ones(RW_LANES, n)
    vec = vec_ref[...]
    row = lambda i: vec[i:i + 1, :]

    def prep0(i, carry):
        t0 = pl.multiple_of(i * RW_PREP, RW_PREP)
        sl = pl.ds(t0, RW_PREP)
        tp = pl.multiple_of(jnp.maximum(t0 - SUBLANES, 0), SUBLANES)

        def mixed(p_ref, c_ref, mu):
            x = p_ref[0, sl, :]
            prev8 = jnp.where(i == 0, c_ref[...], p_ref[0, pl.ds(tp, SUBLANES), :])
            return x + mu * (_shift_rows(x, prev8, 1) - x)

        r = mixed(pr_ref, c_r, row(RV_MU_R))
        k = mixed(pk_ref, c_k, row(RV_MU_K))
        v = mixed(pv_ref, c_v, row(RV_MU_V))
        dg = mixed(pg_ref, c_g, mug_ref[...])
        wa = mixed(pwa_ref, c_wa, muwa_ref[...])
        s_lw[sl, :] = -EXP_M05 * _sigmoid(row(RV_W0) + _dot(jnp.tanh(wa), wup_ref[...]))
        a = _sigmoid(row(RV_A0) + _dot(wa, aup_ref[...]))
        s_a[sl, :] = a
        s_gate[sl, :] = _dot(_sigmoid(dg), gup_ref[...])
        if vres:
            dv = mixed(pdv_ref, c_dv, mudv_ref[...])
            v = v + (vf_ref[0, sl, :] - v) * _sigmoid(row(RV_V0) + _dot(dv, vup_ref[...]))
        else:
            vfo_ref[0, sl, :] = v
        kk = k * row(RV_KK)
        norm = jnp.sqrt(_dot_hilo(kk * kk, seg_ones))
        s_kk[sl, :] = kk / jnp.maximum(norm, 1e-12)
        k = k * (1.0 + (a - 1.0) * row(RV_KA))
        s_r[sl, :] = r
        s_k[sl, :] = k
        s_v[sl, :] = v
        s_bonus[sl, :] = _dot_hilo(r * k * row(RV_RK), seg_ones) * v
        return carry

    lax.fori_loop(0, tb // RW_PREP, prep0, 0)
    for p_ref, c_ref in raws:
        c_ref[...] = p_ref[0, tb - SUBLANES:tb, :]

    rowi = lax.broadcasted_iota(jnp.int32, (c_len, c_len), 0)
    coli = lax.broadcasted_iota(jnp.int32, (c_len, c_len), 1)
    tri_incl = (rowi >= coli).astype(BF16)
    eye = (rowi == coli).astype(F32)
    row2 = lax.broadcasted_iota(jnp.int32, (2 * c_len, 2 * c_len), 0)
    col2 = lax.broadcasted_iota(jnp.int32, (2 * c_len, 2 * c_len), 1)
    rr = jnp.where(row2 >= c_len, row2 - c_len, row2)
    cc = jnp.where(col2 >= c_len, col2 - c_len, col2)
    gmask = (rr - cc) >= jnp.where(row2 >= c_len, 0, 1)

    def prep(c):
        t0 = pl.multiple_of(c * c_len, c_len)
        sl = pl.ds(t0, c_len)
        lw = s_lw[sl, :]
        r = s_r[sl, :]
        k = s_k[sl, :]
        v = s_v[sl, :]
        kk = s_kk[sl, :]
        a = s_a[sl, :]
        lw_hi = lw.astype(BF16)
        lw_lo = (lw - lw_hi.astype(F32)).astype(BF16)
        cum2 = jnp.dot(tri_incl, jnp.concatenate([lw_hi, lw_lo], axis=1), preferred_element_type=F32)
        cum = cum2[:, :RW_LANES] + cum2[:, RW_LANES:]
        cum_prev = cum - lw
        cum_end = cum[c_len - 1:c_len, :]
        g_inv = jnp.exp(-cum)
        to_end = jnp.exp(cum_end - cum)
        beta = kk * a
        al_h = -kk * jnp.exp(cum_prev)
        r_h = r * jnp.exp(cum)
        be_c = beta * g_inv
        k_c = k * g_inv
        be_t = (beta * to_end).T
        k_t = (k * to_end).T
        gc_ref[c] = jnp.exp(cum.T[:, c_len - 1:c_len])
        probs = []
        for h in range(hp):
            hs = slice(h * n, (h + 1) * n)
            probs.append(dict(c=c, h=h, ah=al_h[:, hs], rh=r_h[:, hs], vh=v[:, hs],
                              bc=be_c[:, hs], kc=k_c[:, hs], bt=be_t[hs, :], kt=k_t[hs, :]))
        return probs

    def phase1(i, carry):
        probs = []
        for u in range(RW_UNROLL):
            probs += prep(i * RW_UNROLL + u)
        for q in probs:
            g = _dot_nt(jnp.concatenate([q["ah"], q["rh"]], axis=0),
                        jnp.concatenate([q["bc"], q["kc"]], axis=0))
            g = jnp.where(gmask, g, 0.0)
            q["a_ak"] = g[:c_len, c_len:]
            q["lower"] = g[c_len:, :]
            q["p"] = g[:c_len, :c_len]
            q["t"] = eye + q["p"]
        for _ in range(5):
            for q in probs:
                q["p"] = _dot(q["p"], q["p"])
            for q in probs:
                q["t"] = q["t"] + _dot(q["t"], q["p"])
        for q in probs:
            q["av"] = _dot(q["a_ak"], q["vh"])
        for q in probs:
            q["aw"] = _dot(q["t"], jnp.concatenate([q["ah"], q["av"]], axis=1))
        for q in probs:
            z = jnp.concatenate(
                [q["aw"], jnp.concatenate([jnp.zeros((c_len, n), F32), q["vh"]], axis=1)], axis=0)
            lhs = jnp.concatenate(
                [q["lower"], jnp.concatenate([q["bt"], q["kt"]], axis=1)], axis=0)
            f = _dot(lhs, z)
            rm = f[:, :n] + jnp.concatenate([q["rh"], jnp.zeros((n, n), F32)], axis=0)
            rm_ref[q["c"], q["h"]] = rm.astype(BF16)
            yn_ref[q["c"], q["h"]] = f[:, n:]
        return carry

    lax.fori_loop(0, nc // RW_UNROLL, phase1, 0)

    def phase2(c, carry):
        t0 = pl.multiple_of(c * c_len, c_len)
        gcol = gc_ref[c]
        ss = [st_ref[h] for h in range(hp)]
        fs = [jnp.dot(rm_ref[c, h], ss[h].astype(BF16), preferred_element_type=F32)
              for h in range(hp)]
        ys = []
        for h in range(hp):
            yn = yn_ref[c, h]
            ys.append(fs[h][:c_len] + yn[:c_len])
            st_ref[h] = gcol[h * n:(h + 1) * n] * ss[h] + fs[h][c_len:] + yn[c_len:]
        s_y[pl.ds(t0, c_len), :] = jnp.concatenate(ys, axis=1)
        return carry

    lax.fori_loop(0, nc, phase2, 0)

    def post(i, carry):
        t0 = pl.multiple_of(i * RW_PREP, RW_PREP)
        sl = pl.ds(t0, RW_PREP)
        y = s_y[sl, :]
        mean = _dot_hilo(y, seg_ones) * (1.0 / n)
        yc = y - mean
        var = _dot_hilo(yc * yc, seg_ones) * (1.0 / n)
        yn = yc * lax.rsqrt(var + RWKV_GN_EPS) * row(RV_LNW) + row(RV_LNB)
        ya_ref[0, sl, :] = ((yn + s_bonus[sl, :]) * s_gate[sl, :]).astype(BF16)
        return carry

    lax.fori_loop(0, tb // RW_PREP, post, 0)


def rwkv7_mixer(p, mu, w_up, a_up, g_up, vecs, r_k, v_first, vres_params):
    bsz, seq, _ = p.shape
    vres = vres_params is not None
    tb = SEQ_TB
    nc = tb // RW_C
    hp = RW_LANES // N_A
    pad_rows = lambda w, before, total: jnp.pad(w, ((before, total - before - w.shape[0]), (0, 0))).astype(BF16)
    rows = [mu[:D_A], mu[D_A:2 * D_A], mu[2 * D_A:3 * D_A]] + [vecs[i] for i in range(6)] + [r_k.reshape(D_A)]
    rows.append(vres_params[2] if vres else jnp.zeros((D_A,), F32))
    table = jnp.pad(jnp.stack(rows), ((0, RV_ROWS - len(rows)), (0, 0)))
    mu_wa = mu[3 * D_A:3 * D_A + R_W + R_A].reshape(1, LANES)
    mu_g = jnp.pad(mu[3 * D_A + R_W + R_A:], (0, 256 - R_G)).reshape(1, 256)
    blk = lambda w, f: pl.BlockSpec((1, tb, w), f)
    col = lambda w: pl.BlockSpec((w.shape[0], RW_LANES), lambda b, h, t: (0, h))
    full = lambda w: pl.BlockSpec(w.shape, lambda b, h, t: (0, 0))
    q = RW_LANES
    wup = pad_rows(w_up, 0, LANES)
    aup = pad_rows(a_up, R_W, LANES)
    gup = pad_rows(g_up, 0, 256)
    in_specs = [blk(q, lambda b, h, t: (b, t, _block_index(OFF_RKV, q) + h)),
                blk(q, lambda b, h, t: (b, t, _block_index(OFF_RKV + D_A, q) + h)),
                blk(q, lambda b, h, t: (b, t, _block_index(OFF_RKV + 2 * D_A, q) + h)),
                blk(256, lambda b, h, t: (b, t, _block_index(OFF_G, 256))),
                blk(LANES, lambda b, h, t: (b, t, _block_index(OFF_WA, LANES)))]
    args = [p, p, p, p, p]
    if vres:
        mu_dv = jnp.pad(vres_params[0], (H_B, LANES - H_B - R_V)).reshape(1, LANES)
        vup = pad_rows(vres_params[1], H_B, LANES)
        in_specs += [blk(LANES, lambda b, h, t: (b, t, _block_index(OFF_DT, LANES))),
                     blk(q, lambda b, h, t: (b, t, h))]
        args += [p, v_first]
    in_specs += [col(table), full(mu_wa), full(mu_g)]
    args += [table, mu_wa, mu_g]
    if vres:
        in_specs.append(full(mu_dv))
        args.append(mu_dv)
    in_specs += [col(wup), col(aup), col(gup)]
    args += [wup, aup, gup]
    if vres:
        in_specs.append(col(vup))
        args.append(vup)
    out_blk = blk(q, lambda b, h, t: (b, t, h))
    out_shape = [jax.ShapeDtypeStruct((bsz, seq, D_A), BF16)]
    out_specs = [out_blk]
    if not vres:
        out_shape.append(jax.ShapeDtypeStruct((bsz, seq, D_A), F32))
        out_specs.append(out_blk)
    carries = [pltpu.VMEM((SUBLANES, w), F32) for w in (q, q, q, 256, LANES) + ((LANES,) if vres else ())]
    outs = pl.pallas_call(
        functools.partial(_rwkv_kernel, nc=nc, vres=vres),
        out_shape=tuple(out_shape),
        grid=(bsz, D_A // q, seq // tb),
        in_specs=in_specs,
        out_specs=tuple(out_specs),
        scratch_shapes=[
            pltpu.VMEM((hp, N_A, N_A), F32),
            pltpu.VMEM((nc, hp, RW_C + N_A, N_A), BF16),
            pltpu.VMEM((nc, hp, RW_C + N_A, N_A), F32),
            pltpu.VMEM((nc, q, 1), F32),
        ] + carries + [pltpu.VMEM((tb, q), F32)] * 9,
        compiler_params=_cparams(("parallel", "parallel", "arbitrary")),
        name="rwkv7_mixer",
    )(*args)
    return (outs[0], v_first) if vres else (outs[0], outs[1])


SSD_TB = 256


def _ssd_kernel(z_ref, xs_ref, bm_ref, cm_ref, dt_ref, cw_ref, cb_ref, hd_ref, dsk_ref, ng_ref, o_ref,
                st_ref, c_xs, c_bm, c_cm, s_xs, s_bm, s_cm, s_dt, s_la):
    tb = SSD_TB
    raws = [(xs_ref, c_xs, 0, D_B), (bm_ref, c_bm, D_B, G_B * N_B), (cm_ref, c_cm, D_B + G_B * N_B, G_B * N_B)]

    @pl.when(pl.program_id(1) == 0)
    def _():
        st_ref[...] = jnp.zeros_like(st_ref)
        for _, c_ref, _, _ in raws:
            c_ref[...] = jnp.zeros_like(c_ref)

    for (p_ref, c_ref, off, width), s_ref in zip(raws, (s_xs, s_bm, s_cm)):
        x = p_ref[0]
        prev8 = c_ref[...]
        acc = cb_ref[:, off:off + width] + cw_ref[CONV_K - 1:CONV_K, off:off + width] * x
        for d in range(1, CONV_K):
            acc = acc + cw_ref[CONV_K - 1 - d:CONV_K - d, off:off + width] * _shift_rows(x, prev8, d)
        s_ref[...] = acc * _sigmoid(acc)
        c_ref[...] = p_ref[0, tb - SUBLANES:tb, :]
    dt = _softplus(dt_ref[0] + hd_ref[0:1, :])
    s_dt[...] = dt
    s_la[...] = dt * (-jnp.exp(hd_ref[1:2, :]))

    rowi = lax.broadcasted_iota(jnp.int32, (CHUNK, CHUNK), 0)
    coli = lax.broadcasted_iota(jnp.int32, (CHUNK, CHUNK), 1)
    causal = rowi >= coli
    tri_incl = causal.astype(BF16)
    first_head = coli < P_B
    expand = (lax.shift_right_logical(lax.broadcasted_iota(jnp.int32, (LANES, D_B), 1), P_B.bit_length() - 1)
              == lax.broadcasted_iota(jnp.int32, (LANES, D_B), 0)).astype(BF16)
    pw = 2 * P_B

    def chunk(c, carry):
        t0 = pl.multiple_of(c * CHUNK, CHUNK)
        sl = pl.ds(t0, CHUNK)
        la = s_la[sl, :]
        la_hi = la.astype(BF16)
        la_lo = (la - la_hi.astype(F32)).astype(BF16)
        cum2 = jnp.dot(tri_incl, jnp.concatenate([la_hi, la_lo], axis=1), preferred_element_type=F32)
        cum = cum2[:, :LANES] + cum2[:, LANES:]
        cum_t = cum.T
        ecum_f = _dot_hilo(jnp.exp(cum), expand)
        dend_f = _dot_hilo(jnp.exp(cum[CHUNK - 1:CHUNK, :] - cum), expand)
        xdt_f = s_xs[sl, :] * _dot_hilo(s_dt[sl, :], expand)
        xde_f = xdt_f * dend_f
        outs = []
        for g in range(G_B):
            cmg = s_cm[sl, g * N_B:(g + 1) * N_B]
            bmg = s_bm[sl, g * N_B:(g + 1) * N_B]
            cb = _dot_nt(cmg, bmg)
            bmt = bmg.T
            pairs = [g * (J_B // 2) + m for m in range(J_B // 2)]
            lanes = [slice(q * pw, (q + 1) * pw) for q in pairs]
            sc = [[cb * jnp.exp(jnp.where(causal, cum[:, h:h + 1] - cum_t[h:h + 1, :], -1e30))
                   for h in (2 * q, 2 * q + 1)] for q in pairs]
            sts = [st_ref[q] for q in pairs]
            y_in = [_dot(jnp.concatenate(s2, axis=1),
                         jnp.concatenate([jnp.where(first_head, xdt_f[:, ln], 0.0),
                                          jnp.where(first_head, 0.0, xdt_f[:, ln])], axis=0))
                    for s2, ln in zip(sc, lanes)]
            y_st = [_dot(cmg, st) for st in sts]
            upd = [_dot(bmt, xde_f[:, ln]) for ln in lanes]
            for j, (q, ln) in enumerate(zip(pairs, lanes)):
                st_ref[q] = sts[j] * ecum_f[CHUNK - 1:CHUNK, ln] + upd[j]
                outs.append(y_in[j] + y_st[j] * ecum_f[:, ln])
        y = jnp.concatenate(outs, axis=1)
        zc = z_ref[0, sl, :]
        y = (y + dsk_ref[...] * s_xs[sl, :]) * (zc * _sigmoid(zc))
        pieces = []
        gw = D_B // G_B
        for g in range(G_B):
            yg = y[:, g * gw:(g + 1) * gw]
            pieces.append(yg * lax.rsqrt(jnp.mean(yg * yg, axis=-1, keepdims=True) + MAMBA_NORM_EPS))
        o_ref[0, sl, :] = (jnp.concatenate(pieces, axis=1) * ng_ref[...]).astype(BF16)
        return carry

    lax.fori_loop(0, tb // CHUNK, chunk, 0)


def mamba2_mixer(p, conv_w, conv_b, head, norm_g):
    bsz, seq, _ = p.shape
    tb = SSD_TB
    hd = jnp.pad(head[:2], ((0, SUBLANES - 2), (0, LANES - H_B)))
    dsk = jnp.repeat(head[2], P_B).reshape(1, D_B)
    gw = G_B * N_B
    blk = lambda w, off: pl.BlockSpec((1, tb, w), lambda b, t: (b, t, _block_index(off, w)))
    full = lambda a: pl.BlockSpec(a.shape, lambda b, t: (0,) * a.ndim)
    cb = conv_b.reshape(1, C_CONV)
    ng = norm_g.reshape(1, D_B)
    return pl.pallas_call(
        _ssd_kernel,
        out_shape=jax.ShapeDtypeStruct((bsz, seq, D_B), BF16),
        grid=(bsz, seq // tb),
        in_specs=[blk(D_B, OFF_Z), blk(D_B, OFF_XS), blk(gw, OFF_BM), blk(gw, OFF_CM), blk(LANES, OFF_DT),
                  full(conv_w), full(cb), full(hd), full(dsk), full(ng)],
        out_specs=pl.BlockSpec((1, tb, D_B), lambda b, t: (b, t, 0)),
        scratch_shapes=[
            pltpu.VMEM((H_B // 2, N_B, 2 * P_B), F32),
            pltpu.VMEM((SUBLANES, D_B), F32), pltpu.VMEM((SUBLANES, gw), F32), pltpu.VMEM((SUBLANES, gw), F32),
            pltpu.VMEM((tb, D_B), F32), pltpu.VMEM((tb, gw), F32), pltpu.VMEM((tb, gw), F32),
            pltpu.VMEM((tb, LANES), F32), pltpu.VMEM((tb, LANES), F32),
        ],
        compiler_params=_cparams(("parallel", "arbitrary")),
        name="mamba2_mixer",
    )(p, p, p, p, p, conv_w, cb, hd, dsk, ng)


def _lru_kernel(gate_ref, xr_ref, cw_ref, cb_ref, gw_ref, gb_ref, lam_ref, o_ref, c_x, h_ref):
    tb = SEQ_TB
    first = pl.program_id(2) == 0

    @pl.when(first)
    def _():
        c_x[...] = jnp.zeros_like(c_x)
        h_ref[...] = jnp.zeros_like(h_ref)

    x = xr_ref[0]
    prev8 = c_x[...]
    xb = cb_ref[...] + cw_ref[CONV_K - 1:CONV_K, :] * x
    for d in range(1, CONV_K):
        xb = xb + cw_ref[CONV_K - 1 - d:CONV_K - d, :] * _shift_rows(x, prev8, d)
    c_x[...] = xr_ref[0, tb - SUBLANES:tb, :]
    gates = _sigmoid(_dot(xb, gw_ref[0]) + gb_ref[0])
    r_gate, i_gate = gates[:, :BS_C], gates[:, BS_C:]
    log_a = -C_RG * r_gate * _softplus(-lam_ref[...])
    a = jnp.exp(log_a)
    mult = jnp.sqrt(-jnp.tanh(log_a) * (a * a + 1.0))
    rowi = lax.broadcasted_iota(jnp.int32, (tb, BS_C), 0)
    mult = jnp.where(jnp.logical_and(first, rowi == 0), 1.0, mult)
    b = mult * (i_gate * xb)
    d = 1
    while d < tb:
        if d < SUBLANES:
            a_s = jnp.where(rowi >= d, pltpu.roll(a, d, axis=0), 1.0)
            b_s = jnp.where(rowi >= d, pltpu.roll(b, d, axis=0), 0.0)
        else:
            a_s = jnp.concatenate([jnp.ones((d, BS_C), F32), a[:tb - d]], axis=0)
            b_s = jnp.concatenate([jnp.zeros((d, BS_C), F32), b[:tb - d]], axis=0)
        b = b + a * b_s
        a = a * a_s
        d *= 2
    h = b + a * h_ref[...]
    h_ref[...] = h[tb - 1:tb, :]
    g = gate_ref[0]
    gelu = 0.5 * g * (1.0 + jnp.tanh(0.7978845608028654 * (g + 0.044715 * (g * g * g))))
    o_ref[0] = (h * gelu).astype(BF16)


def rglru_mixer(p, conv_w, conv_b, gate_w, gate_b, lam):
    bsz, seq, _ = p.shape
    tb = SEQ_TB
    gw = jnp.concatenate([gate_w[0], gate_w[1]], axis=-1).astype(BF16)
    gb = jnp.concatenate([gate_b[0].reshape(H_C, 1, BS_C), gate_b[1].reshape(H_C, 1, BS_C)], axis=-1)
    vec = lambda a: pl.BlockSpec((a.shape[0], BS_C), lambda b, h, t: (0, h))
    cb = conv_b.reshape(1, D_RNN)
    lm = lam.reshape(1, D_RNN)
    return pl.pallas_call(
        _lru_kernel,
        out_shape=jax.ShapeDtypeStruct((bsz, seq, D_RNN), BF16),
        grid=(bsz, H_C, seq // tb),
        in_specs=[pl.BlockSpec((1, tb, BS_C), lambda b, h, t: (b, t, h)),
                  pl.BlockSpec((1, tb, BS_C), lambda b, h, t: (b, t, H_C + h)),
                  vec(conv_w), vec(cb),
                  pl.BlockSpec((1, BS_C, 2 * BS_C), lambda b, h, t: (h, 0, 0)),
                  pl.BlockSpec((1, 1, 2 * BS_C), lambda b, h, t: (h, 0, 0)),
                  vec(lm)],
        out_specs=pl.BlockSpec((1, tb, BS_C), lambda b, h, t: (b, t, h)),
        scratch_shapes=[pltpu.VMEM((SUBLANES, BS_C), F32), pltpu.VMEM((1, BS_C), F32)],
        compiler_params=_cparams(("parallel", "parallel", "arbitrary")),
        name="rglru_mixer",
    )(p, p, conv_w, cb, gw, gb, lm)


def _moe_ffn_kernel(be_ref, x_ref, wg_ref, wu_ref, wd_ref, o_ref, wgu_s, wd_s):
    i = pl.program_id(0)

    @pl.when(jnp.logical_or(i == 0, be_ref[i] != be_ref[jnp.maximum(i - 1, 0)]))
    def _():
        wgu_s[:, :D_EXPERT] = wg_ref[0, 0].astype(BF16)
        wgu_s[:, D_EXPERT:] = wu_ref[0, 0].astype(BF16)
        wd_s[...] = wd_ref[0, 0].astype(BF16)

    gu = jnp.dot(x_ref[...], wgu_s[...], preferred_element_type=F32)
    g, u = gu[:, :D_EXPERT], gu[:, D_EXPERT:]
    hb = (g * jax.nn.sigmoid(g)) * u
    o_ref[...] = jnp.dot(hb.astype(BF16), wd_s[...], preferred_element_type=F32).astype(BF16)


def moe_ffn(block_exp, xg, w_gate, w_up, w_down, layer):
    n_slots, d = xg.shape
    grid_spec = pltpu.PrefetchScalarGridSpec(
        num_scalar_prefetch=1,
        grid=(n_slots // MOE_BLOCK,),
        in_specs=[pl.BlockSpec((MOE_BLOCK, d), lambda i, be: (i, 0)),
                  pl.BlockSpec((1, 1, d, D_EXPERT), lambda i, be: (layer, be[i], 0, 0)),
                  pl.BlockSpec((1, 1, d, D_EXPERT), lambda i, be: (layer, be[i], 0, 0)),
                  pl.BlockSpec((1, 1, D_EXPERT, d), lambda i, be: (layer, be[i], 0, 0))],
        out_specs=pl.BlockSpec((MOE_BLOCK, d), lambda i, be: (i, 0)),
        scratch_shapes=[pltpu.VMEM((d, 2 * D_EXPERT), BF16), pltpu.VMEM((D_EXPERT, d), BF16)],
    )
    return pl.pallas_call(
        _moe_ffn_kernel,
        out_shape=jax.ShapeDtypeStruct((n_slots, d), BF16),
        grid_spec=grid_spec,
        compiler_params=_cparams(("arbitrary",)),
        name="moe_ffn",
    )(block_exp, xg, w_gate, w_up, w_down)


def hier_moe(h, norm_g, w_grp, b_grp, w_exp, b_exp, w_gate, w_up, w_down, layer):
    t, d = h.shape
    w_r = jnp.pad(jnp.concatenate([w_grp, w_exp], axis=1), ((0, 0), (0, LANES - N_GROUPS_E - N_EXPERTS)))
    xn, logits = norm_router(h, norm_g, w_r.astype(BF16))
    g_logits = logits[:, :N_GROUPS_E] + b_grp.astype(F32)
    g_idx = jnp.argmax(g_logits, axis=-1)
    g_w = jnp.take_along_axis(jax.nn.softmax(g_logits, axis=-1), g_idx[:, None], axis=-1)
    e_logits = (logits[:, N_GROUPS_E:N_GROUPS_E + N_EXPERTS] + b_exp.astype(F32)).reshape(
        t, N_GROUPS_E, E_PER_GROUP)
    e_logits = jnp.take_along_axis(e_logits, g_idx[:, None, None], axis=1)[:, 0]
    e_top, e_loc = lax.top_k(e_logits, TOP_K_E)
    e_w = jax.nn.softmax(e_top, axis=-1) * g_w
    e_flat = (g_idx[:, None] * E_PER_GROUP + e_loc).reshape(-1).astype(jnp.int32)
    n_assign = t * TOP_K_E
    onehot = (e_flat[:, None] == jnp.arange(N_EXPERTS, dtype=jnp.int32)[None, :]).astype(jnp.int32)
    running = jnp.cumsum(onehot, axis=0)
    counts = running[-1]
    rank = jnp.take_along_axis(running, e_flat[:, None], axis=1)[:, 0] - 1
    padded = (counts + MOE_BLOCK - 1) // MOE_BLOCK * MOE_BLOCK
    raw_start = jnp.cumsum(counts) - counts
    pad_start = jnp.cumsum(padded) - padded
    dest = pad_start[e_flat] + rank
    n_slots = n_assign + N_EXPERTS * MOE_BLOCK
    n_blocks = n_slots // MOE_BLOCK
    block_start = jnp.arange(n_blocks, dtype=jnp.int32) * MOE_BLOCK
    block_exp = jnp.minimum(
        jnp.sum(block_start[:, None] >= (pad_start + padded)[None, :], axis=-1), N_EXPERTS - 1)
    order = jnp.argsort(e_flat)
    off = (block_start - pad_start[block_exp])[:, None] + jnp.arange(MOE_BLOCK, dtype=jnp.int32)[None, :]
    src = order[jnp.clip(raw_start[block_exp][:, None] + off, 0, n_assign - 1)]
    spread = jnp.arange(n_slots, dtype=jnp.int32).reshape(n_blocks, MOE_BLOCK) % t
    slot_tok = jnp.where(off < counts[block_exp][:, None], src // TOP_K_E, spread).reshape(n_slots)
    y = moe_ffn(block_exp.astype(jnp.int32), xn[slot_tok], w_gate, w_up, w_down, layer)
    slot_of = dest.reshape(t, TOP_K_E)
    return h + (y[slot_of[:, 0]] * e_w[:, 0:1] + y[slot_of[:, 1]] * e_w[:, 1:2])


def _hybrid_w_in(w_in):
    d = w_in.shape[0]
    zeros = lambda n: jnp.zeros((d, n), w_in.dtype)
    c_wa, c_g = 3 * D_A, 3 * D_A + R_W + R_A
    dv = w_in[:, C_HY:C_HY + R_V] if w_in.shape[1] > C_HY else zeros(R_V)
    c_x = C_A + D_B
    return jnp.concatenate([
        w_in[:, C_A:c_x], w_in[:, c_x:c_x + D_B], w_in[:, :3 * D_A], w_in[:, c_x + D_B:c_x + C_CONV],
        w_in[:, c_g:C_A], zeros(256 - R_G), w_in[:, c_wa:c_g],
        w_in[:, c_x + C_CONV:C_HY], dv, zeros(LANES - H_B - R_V)], axis=1).astype(BF16)


def kernel(x, norm_mix, norm_ffn, norm_final, hy_w_in_first, hy_w_in_vres, rk_mu, rk_mu_v, rk_w_up, rk_a_up, rk_g_up, rk_vecs, rk_r_k, rk_v_up, rk_v0, mb_conv_w, mb_conv_b, mb_head, mb_norm, hy_w_out, lr_w_in, lr_conv_w, lr_conv_b, lr_gate_w, lr_gate_b, lr_lambda, lr_w_out, moe_w_grp, moe_b_grp, moe_w_exp, moe_b_exp, moe_w_gate, moe_w_up, moe_w_down):
    bsz, seq, d = x.shape
    t = bsz * seq
    h = x.reshape(t, d)
    v_first = None
    for layer in range(DEPTH):
        j = layer // 2
        if layer % 2 == 0:
            w_in = hy_w_in_first if j == 0 else hy_w_in_vres[j - 1]
            vres_params = None if j == 0 else (rk_mu_v[j - 1], rk_v_up[j - 1], rk_v0[j - 1])
            p = norm_matmul(h, norm_mix[layer], _hybrid_w_in(w_in)).reshape(bsz, seq, P_W)
            ya, v_first = rwkv7_mixer(p, rk_mu[j], rk_w_up[j], rk_a_up[j], rk_g_up[j], rk_vecs[j], rk_r_k[j],
                                      v_first, vres_params)
            yb = mamba2_mixer(p, mb_conv_w[j], mb_conv_b[j], mb_head[j], mb_norm[j])
            w_out = hy_w_out[j].astype(BF16)
            h = matmul_residual([ya.reshape(t, D_A), yb.reshape(t, D_B)], [w_out[:D_A], w_out[D_A:]], h)
        else:
            p = norm_matmul(h, norm_mix[layer], lr_w_in[j].astype(BF16)).reshape(bsz, seq, 2 * D_RNN)
            y = rglru_mixer(p, lr_conv_w[j], lr_conv_b[j], lr_gate_w[j], lr_gate_b[j], lr_lambda[j])
            h = matmul_residual([y.reshape(t, D_RNN)], [lr_w_out[j].astype(BF16)], h)
        h = hier_moe(h, norm_ffn[layer], moe_w_grp[layer], moe_b_grp[layer], moe_w_exp[layer],
                     moe_b_exp[layer], moe_w_gate, moe_w_up, moe_w_down, layer)
    return final_norm(h, norm_final).reshape(bsz, seq, d)
```
